```python
import jax, jax.numpy as jnp
from jax import lax
import numpy as np

D_MODEL = 2048
BATCH = 4
SEQ = 4096
DEPTH = 2

HEAD_DIM = 128
N_SB_HEADS = D_MODEL // (2 * HEAD_DIM)
N_FOX_HEADS = D_MODEL // (2 * HEAD_DIM)
SB_WIDTH = N_SB_HEADS * HEAD_DIM
FOX_WIDTH = N_FOX_HEADS * HEAD_DIM
IN_WIDTH = 3 * SB_WIDTH + 3 * FOX_WIDTH + N_FOX_HEADS

N_MLA_HEADS = 16
Q_LORA_RANK = 512
KV_LORA_RANK = 512
QK_NOPE_DIM = 128
QK_ROPE_DIM = 64
QK_HEAD_DIM = QK_NOPE_DIM + QK_ROPE_DIM
V_HEAD_DIM = 128
DOWN_WIDTH = Q_LORA_RANK + KV_LORA_RANK + QK_ROPE_DIM

D_FF = 4 * D_MODEL
BLOCK_Q = 128
ROPE_THETA = 10000.0
EPS = 1e-6
FORGET_BIAS_CENTER = 3.0

N_EVEN = (DEPTH + 1) // 2
N_ODD = DEPTH // 2

kernel_name = "hybrid_stickbreak_fox_mla_sqrelu"


def rmsnorm(x, g):
    x32 = x.astype(jnp.float32)
    y = x32 * lax.rsqrt(jnp.mean(x32 * x32, axis=-1, keepdims=True) + EPS)
    return (y * g.astype(jnp.float32)).astype(x.dtype)


def to_heads(t, n_heads):
    b, s, _ = t.shape
    return t.reshape(b, s, n_heads, -1).transpose(0, 2, 1, 3)


def query_blocks(q):
    b, h, s, d = q.shape
    return q.reshape(b, h, s // BLOCK_Q, BLOCK_Q, d).transpose(2, 0, 1, 3, 4)


def merge_blocks(o):
    nb, b, h, bq, d = o.shape
    return o.transpose(1, 0, 3, 2, 4).reshape(b, nb * bq, h * d)


def stick_breaking_attention(q, k, v):
    s = q.shape[2]
    scale = q.shape[-1] ** -0.5
    key_pos = jnp.arange(s)

    def one_block(args):
        qb, q0 = args
        z = jnp.einsum('bhqd,bhkd->bhqk', qb, k,
                       preferred_element_type=jnp.float32) * scale
        q_pos = q0 + jnp.arange(BLOCK_Q)
        strict = key_pos[None, :] < q_pos[:, None]
        log_beta = jax.nn.log_sigmoid(z)
        log_one_minus = jnp.where(strict, jax.nn.log_sigmoid(-z), 0.0)
        suffix = lax.cumsum(log_one_minus, axis=3, reverse=True) - log_one_minus
        w = jnp.where(strict, jnp.exp(log_beta + suffix), 0.0)
        return jnp.einsum('bhqk,bhkd->bhqd', w.astype(v.dtype), v,
                          preferred_element_type=jnp.float32).astype(v.dtype)

    starts = jnp.arange(s // BLOCK_Q, dtype=jnp.int32) * BLOCK_Q
    return merge_blocks(lax.map(one_block, (query_blocks(q), starts)))


def causal_softmax_attention(q, k, v, log_decay_cum=None):
    s = q.shape[2]
    scale = q.shape[-1] ** -0.5
    key_pos = jnp.arange(s)

    def one_block(args):
        qb, q0 = args
        logits = jnp.einsum('bhqd,bhkd->bhqk', qb, k,
                            preferred_element_type=jnp.float32) * scale
        if log_decay_cum is not None:
            f_q = lax.dynamic_slice_in_dim(log_decay_cum, q0, BLOCK_Q, axis=2)
            logits = logits + f_q[..., :, None] - log_decay_cum[..., None, :]
        q_pos = q0 + jnp.arange(BLOCK_Q)
        causal = key_pos[None, :] <= q_pos[:, None]
        p = jax.nn.softmax(jnp.where(causal, logits, -jnp.inf), axis=-1)
        return jnp.einsum('bhqk,bhkd->bhqd', p.astype(v.dtype), v,
                          preferred_element_type=jnp.float32).astype(v.dtype)

    starts = jnp.arange(s // BLOCK_Q, dtype=jnp.int32) * BLOCK_Q
    return merge_blocks(lax.map(one_block, (query_blocks(q), starts)))


def apply_rope(t, positions):
    half = t.shape[-1] // 2
    inv_freq = ROPE_THETA ** (-jnp.arange(half, dtype=jnp.float32) / half)
    ang = positions.astype(jnp.float32)[:, None, :, None] * inv_freq
    cos, sin = jnp.cos(ang), jnp.sin(ang)
    t32 = t.astype(jnp.float32)
    t1, t2 = t32[..., :half], t32[..., half:]
    return jnp.concatenate([t1 * cos - t2 * sin, t2 * cos + t1 * sin], axis=-1).astype(t.dtype)


def sb_fox_mixer(h, w_in, b_f, fox_q_g, fox_k_g, w_o):
    proj = h @ w_in
    cuts = np.cumsum([SB_WIDTH] * 3 + [FOX_WIDTH] * 3)
    q_sb, k_sb, v_sb, q_fx, k_fx, v_fx, f_logit = jnp.split(proj, cuts, axis=-1)
    o_sb = stick_breaking_attention(to_heads(q_sb, N_SB_HEADS),
                                    to_heads(k_sb, N_SB_HEADS),
                                    to_heads(v_sb, N_SB_HEADS))
    log_f = jax.nn.log_sigmoid(f_logit.astype(jnp.float32) + b_f.astype(jnp.float32))
    log_f_cum = jnp.cumsum(log_f, axis=1).transpose(0, 2, 1)
    q_f = rmsnorm(to_heads(q_fx, N_FOX_HEADS), fox_q_g)
    k_f = rmsnorm(to_heads(k_fx, N_FOX_HEADS), fox_k_g)
    o_fx = causal_softmax_attention(q_f, k_f, to_heads(v_fx, N_FOX_HEADS), log_f_cum)
    return jnp.concatenate([o_sb, o_fx], axis=-1) @ w_o


def mla_mixer(h, positions, w_down, q_a_g, kv_a_g, w_uq, w_ukv, q_g, k_g, w_o):
    b, s, _ = h.shape
    down = h @ w_down
    c_q, c_kv, k_pe = jnp.split(down, [Q_LORA_RANK, Q_LORA_RANK + KV_LORA_RANK], axis=-1)
    c_q = rmsnorm(c_q, q_a_g)
    c_kv = rmsnorm(c_kv, kv_a_g)
    q = to_heads(c_q @ w_uq, N_MLA_HEADS)
    kv = to_heads(c_kv @ w_ukv, N_MLA_HEADS)
    k_nope, v = kv[..., :QK_NOPE_DIM], kv[..., QK_NOPE_DIM:]
    k_pe = jnp.broadcast_to(k_pe[:, None], (b, N_MLA_HEADS, s, QK_ROPE_DIM))
    k = jnp.concatenate([k_nope, k_pe], axis=-1)
    q = rmsnorm(q, q_g)
    k = rmsnorm(k, k_g)
    q = jnp.concatenate([q[..., :QK_NOPE_DIM], apply_rope(q[..., QK_NOPE_DIM:], positions)], axis=-1)
    k = jnp.concatenate([k[..., :QK_NOPE_DIM], apply_rope(k[..., QK_NOPE_DIM:], positions)], axis=-1)
    o = causal_softmax_attention(q, k, v)
    return o @ w_o


def squared_relu_mlp(h, w_up, w_down):
    a = jnp.square(jax.nn.relu(h @ w_up))
    return a @ w_down


def setup_inputs(seed: int = 0) -> dict:
    key = jax.random.key(seed)
    ks = jax.random.split(key, 24)

    def dense(k, shape, fan_in):
        return jax.random.normal(k, shape, jnp.float32) * (fan_in ** -0.5)

    def gain(k, shape):
        return 1.0 + 0.02 * jax.random.normal(k, shape, jnp.float32)

    x = jax.random.normal(ks[0], (BATCH, SEQ, D_MODEL), jnp.float32)
    offsets = jax.random.randint(ks[1], (BATCH, 1), 0, 2048, dtype=jnp.int32)
    positions = offsets + jnp.arange(SEQ, dtype=jnp.int32)[None, :]
    return {
        "x": x,
        "positions": positions,
        "ln_mix_g": gain(ks[2], (DEPTH, D_MODEL)),
        "ln_mlp_g": gain(ks[3], (DEPTH, D_MODEL)),
        "sf_w_in": dense(ks[4], (N_EVEN, D_MODEL, IN_WIDTH), D_MODEL),
        "sf_b_f": FORGET_BIAS_CENTER + 0.1 * jax.random.normal(ks[5], (N_EVEN, N_FOX_HEADS), jnp.float32),
        "fox_q_g": gain(ks[6], (N_EVEN, HEAD_DIM)),
        "fox_k_g": gain(ks[7], (N_EVEN, HEAD_DIM)),
        "sf_w_o": dense(ks[8], (N_EVEN, SB_WIDTH + FOX_WIDTH, D_MODEL), SB_WIDTH + FOX_WIDTH),
        "mla_w_down": dense(ks[9], (N_ODD, D_MODEL, DOWN_WIDTH), D_MODEL),
        "mla_q_a_g": gain(ks[10], (N_ODD, Q_LORA_RANK)),
        "mla_kv_a_g": gain(ks[11], (N_ODD, KV_LORA_RANK)),
        "mla_w_uq": dense(ks[12], (N_ODD, Q_LORA_RANK, N_MLA_HEADS * QK_HEAD_DIM), Q_LORA_RANK),
        "mla_w_ukv": dense(ks[13], (N_ODD, KV_LORA_RANK, N_MLA_HEADS * (QK_NOPE_DIM + V_HEAD_DIM)), KV_LORA_RANK),
        "mla_q_g": gain(ks[14], (N_ODD, QK_HEAD_DIM)),
        "mla_k_g": gain(ks[15], (N_ODD, QK_HEAD_DIM)),
        "mla_w_o": dense(ks[16], (N_ODD, N_MLA_HEADS * V_HEAD_DIM, D_MODEL), N_MLA_HEADS * V_HEAD_DIM),
        "mlp_w_up": dense(ks[17], (DEPTH, D_MODEL, D_FF), D_MODEL),
        "mlp_w_down": dense(ks[18], (DEPTH, D_FF, D_MODEL), D_FF),
    }


def reference(x, positions, ln_mix_g, ln_mlp_g, sf_w_in, sf_b_f, fox_q_g, fox_k_g, sf_w_o,
              mla_w_down, mla_q_a_g, mla_kv_a_g, mla_w_uq, mla_w_ukv, mla_q_g, mla_k_g,
              mla_w_o, mlp_w_up, mlp_w_down):
    for layer in range(DEPTH):
        i = layer // 2
        h = rmsnorm(x, ln_mix_g[layer])
        if layer % 2 == 0:
            x = x + sb_fox_mixer(h, sf_w_in[i], sf_b_f[i], fox_q_g[i], fox_k_g[i], sf_w_o[i])
        else:
            x = x + mla_mixer(h, positions, mla_w_down[i], mla_q_a_g[i], mla_kv_a_g[i],
                              mla_w_uq[i], mla_w_ukv[i], mla_q_g[i], mla_k_g[i], mla_w_o[i])
        h = rmsnorm(x, ln_mlp_g[layer])
        x = x + squared_relu_mlp(h, mlp_w_up[layer], mlp_w_down[layer])
    return x
```

```python
import functools

import jax
import jax.numpy as jnp
import numpy as np
from jax import lax
from jax.experimental import pallas as pl
from jax.experimental.pallas import tpu as pltpu

F32 = jnp.float32
BF16 = jnp.bfloat16

HEAD_DIM = 128
N_MLA_HEADS = 16
Q_LORA_RANK = 512
KV_LORA_RANK = 512
QK_NOPE_DIM = 128
QK_ROPE_DIM = 64
QK_HEAD_DIM = QK_NOPE_DIM + QK_ROPE_DIM
V_HEAD_DIM = 128
ROPE_THETA = 10000.0
EPS = 1e-6

LANES = 128
VMEM_LIMIT = 56 * 1024 * 1024
ARB = "arbitrary"


def _params(n_axes):
    return pltpu.CompilerParams(dimension_semantics=(ARB,) * n_axes,
                                vmem_limit_bytes=VMEM_LIMIT)


def _rms(x, g):
    ms = jnp.mean(x * x, axis=-1, keepdims=True)
    return x * lax.rsqrt(ms + EPS) * g


def _rms_matmul_kernel(x_ref, g_ref, w_ref, *rest, relu2, aux):
    if aux:
        wa_ref, o_ref, oa_ref, h_scr = rest
    else:
        o_ref, h_scr = rest

    @pl.when(pl.program_id(1) == 0)
    def _():
        h = _rms(x_ref[...].astype(F32), g_ref[...]).astype(BF16)
        h_scr[...] = h
        if aux:
            oa_ref[...] = jnp.dot(h, wa_ref[...], preferred_element_type=F32)

    acc = jnp.dot(h_scr[...], w_ref[...], preferred_element_type=F32)
    if relu2:
        acc = jnp.square(jnp.maximum(acc, 0.0))
    o_ref[...] = acc.astype(o_ref.dtype)


def _rms_matmul(x, g, w, *, xcol=0, out_dtype=BF16, relu2=False, w_aux=None,
                tm=1024, tn=1024, name):
    t = x.shape[0]
    k, n = w.shape
    tm, tn = min(tm, t), min(tn, n)
    assert t % tm == 0 and n % tn == 0
    aux = w_aux is not None
    in_specs = [pl.BlockSpec((tm, k), lambda i, j: (i, xcol)),
                pl.BlockSpec((1, k), lambda i, j: (0, 0)),
                pl.BlockSpec((k, tn), lambda i, j: (0, j))]
    out_specs = pl.BlockSpec((tm, tn), lambda i, j: (i, j))
    out_shape = jax.ShapeDtypeStruct((t, n), out_dtype)
    args = [x, g.reshape(1, k).astype(F32), w]
    if aux:
        na = w_aux.shape[1]
        in_specs.append(pl.BlockSpec((k, na), lambda i, j: (0, 0)))
        out_specs = [out_specs, pl.BlockSpec((tm, na), lambda i, j: (i, 0))]
        out_shape = [out_shape, jax.ShapeDtypeStruct((t, na), F32)]
        args.append(w_aux)
    return pl.pallas_call(
        functools.partial(_rms_matmul_kernel, relu2=relu2, aux=aux),
        grid=(t // tm, n // tn),
        in_specs=in_specs, out_specs=out_specs, out_shape=out_shape,
        scratch_shapes=[pltpu.VMEM((tm, k), BF16)],
        compiler_params=_params(2), name=name)(*args)


def _mm_res_kernel(*refs, n_pairs):
    a_refs = refs[:n_pairs]
    w_refs = refs[n_pairs:2 * n_pairs]
    r_ref, o_ref = refs[2 * n_pairs:]
    k = pl.program_id(2)

    @pl.when(k == 0)
    def _():
        o_ref[...] = r_ref[...]

    acc = jnp.dot(a_refs[0][...], w_refs[0][...], preferred_element_type=F32)
    for a_ref, w_ref in zip(a_refs[1:], w_refs[1:]):
        acc += jnp.dot(a_ref[...], w_ref[...], preferred_element_type=F32)
    o_ref[...] += acc


def _mm_res(a_list, w_list, r, *, tm=1024, tn=1024, tk=1024, name):
    t, n = r.shape
    k = a_list[0].shape[1]
    tm, tn, tk = min(tm, t), min(tn, n), min(tk, k)
    assert t % tm == 0 and n % tn == 0 and k % tk == 0
    n_pairs = len(a_list)
    in_specs = ([pl.BlockSpec((tm, tk), lambda i, j, kk: (i, kk))] * n_pairs
                + [pl.BlockSpec((tk, tn), lambda i, j, kk: (kk, j))] * n_pairs
                + [pl.BlockSpec((tm, tn), lambda i, j, kk: (i, j))])
    return pl.pallas_call(
        functools.partial(_mm_res_kernel, n_pairs=n_pairs),
        grid=(t // tm, n // tn, k // tk),
        in_specs=in_specs,
        out_specs=pl.BlockSpec((tm, tn), lambda i, j, kk: (i, j)),
        out_shape=jax.ShapeDtypeStruct((t, n), F32),
        compiler_params=_params(3), name=name)(*a_list, *w_list, r)


def _log_sigmoid(z):
    return jnp.minimum(z, 0.0) - jnp.log(1.0 + jnp.exp(-jnp.abs(z)))


def _split3(x):
    x1 = x.astype(BF16)
    r1 = x - x1.astype(F32)
    x2 = r1.astype(BF16)
    x3 = (r1 - x2.astype(F32)).astype(BF16)
    return x1, x2, x3


def _forget_cumsum_kernel(f_ref, b_ref, o_ref, *, n_heads, chunk):
    s = f_ref.shape[0]
    lf = _log_sigmoid(f_ref[...] + b_ref[...])
    lft = lf.T[:8, :]
    row = lax.broadcasted_iota(jnp.int32, (chunk, chunk), 0)
    col = lax.broadcasted_iota(jnp.int32, (chunk, chunk), 1)
    upper = jnp.where(row <= col, 1.0, 0.0).astype(BF16)
    carry = jnp.zeros((8, 1), F32)
    for c in range(s // chunk):
        x = lft[:, c * chunk:(c + 1) * chunk]
        cs = carry
        for part in _split3(x):
            cs = cs + jnp.dot(part, upper, preferred_element_type=F32)
        for h in range(n_heads):
            o_ref[h, :, c * chunk:(c + 1) * chunk] = cs[h:h + 1, :]
        carry = cs[:, chunk - 1:chunk]


def _forget_cumsum(f_logit, b_pad, batch, seq, n_heads):
    return pl.pallas_call(
        functools.partial(_forget_cumsum_kernel, n_heads=n_heads, chunk=min(512, seq)),
        grid=(batch,),
        in_specs=[pl.BlockSpec((seq, LANES), lambda b: (b, 0)),
                  pl.BlockSpec((1, LANES), lambda b: (0, 0))],
        out_specs=pl.BlockSpec((None, n_heads, 1, seq), lambda b: (b, 0, 0, 0)),
        out_shape=jax.ShapeDtypeStruct((batch, n_heads, 1, seq), F32),
        compiler_params=_params(1), name="forget_cumsum")(f_logit, b_pad)


def _softmax_attention(qb, k_scr, v_ref, f_ref, o_ref, m_scr, l_scr, acc_scr, *, blk):
    qi = pl.program_id(2)
    m_scr[...] = jnp.full(m_scr.shape, -jnp.inf, F32)
    l_scr[...] = jnp.zeros(l_scr.shape, F32)
    acc_scr[...] = jnp.zeros(acc_scr.shape, F32)

    def chunk(c, masked):
        rows = pl.ds(pl.multiple_of(c * blk, blk), blk)
        s = lax.dot_general(qb, k_scr[rows, :], (((1,), (1,)), ((), ())),
                            preferred_element_type=F32)
        if f_ref is not None:
            s = s - f_ref[:, rows]
        if masked:
            r = lax.broadcasted_iota(jnp.int32, s.shape, 0)
            cc = lax.broadcasted_iota(jnp.int32, s.shape, 1)
            s = jnp.where(cc <= r, s, -jnp.inf)
        m_prev = m_scr[...]
        m_new = jnp.maximum(m_prev, jnp.max(s, axis=-1, keepdims=True))
        alpha = jnp.exp(m_prev - m_new)
        p = jnp.exp(s - m_new)
        l_scr[...] = alpha * l_scr[...] + jnp.sum(p, axis=-1, keepdims=True)
        acc_scr[...] = alpha * acc_scr[...] + jnp.dot(
            p.astype(BF16), v_ref[rows, :], preferred_element_type=F32)
        m_scr[...] = m_new

    def body(c, carry):
        chunk(c, False)
        return carry

    lax.fori_loop(0, qi, body, 0)
    chunk(qi, True)
    o_ref[...] = (acc_scr[...] / l_scr[...]).astype(o_ref.dtype)


def _attn_scratch(blk, dv):
    return [pltpu.VMEM((blk, 1), F32), pltpu.VMEM((blk, 1), F32),
            pltpu.VMEM((blk, dv), F32)]


def _fox_kernel(q_ref, k_ref, v_ref, f_ref, gq_ref, gk_ref, o_ref,
                k_scr, m_scr, l_scr, acc_scr, *, blk, scale):
    @pl.when(pl.program_id(2) == 0)
    def _():
        def prep(c, carry):
            rows = pl.ds(pl.multiple_of(c * blk, blk), blk)
            k_scr[rows, :] = _rms(k_ref[rows, :].astype(F32), gk_ref[...]).astype(BF16)
            return carry
        lax.fori_loop(0, k_ref.shape[0] // blk, prep, 0)

    qb = (_rms(q_ref[...].astype(F32), gq_ref[...]) * scale).astype(BF16)
    _softmax_attention(qb, k_scr, v_ref, f_ref, o_ref, m_scr, l_scr, acc_scr, blk=blk)


def _fox_attention(proj, f_cum, gq, gk, *, n_heads, q_col, k_col, v_col, blk=512):
    b, s, _ = proj.shape
    d = HEAD_DIM
    blk = min(blk, s)
    kv_spec = lambda col: pl.BlockSpec((None, s, d), lambda bi, h, qi: (bi, 0, col + h))
    return pl.pallas_call(
        functools.partial(_fox_kernel, blk=blk, scale=d ** -0.5),
        grid=(b, n_heads, s // blk),
        in_specs=[pl.BlockSpec((None, blk, d), lambda bi, h, qi: (bi, qi, q_col + h)),
                  kv_spec(k_col), kv_spec(v_col),
                  pl.BlockSpec((None, None, 1, s), lambda bi, h, qi: (bi, h, 0, 0)),
                  pl.BlockSpec((1, d), lambda bi, h, qi: (0, 0)),
                  pl.BlockSpec((1, d), lambda bi, h, qi: (0, 0))],
        out_specs=pl.BlockSpec((None, blk, d), lambda bi, h, qi: (bi, qi, h)),
        out_shape=jax.ShapeDtypeStruct((b, s, n_heads * d), BF16),
        scratch_shapes=[pltpu.VMEM((s, d), BF16)] + _attn_scratch(blk, d),
        compiler_params=_params(3), name="fox_attention")(
            proj, proj, proj, f_cum, gq.reshape(1, d).astype(F32), gk.reshape(1, d).astype(F32))


def _mla_norm_rope(nope, pk, tab, g_nope, g_pk):
    lane = lax.broadcasted_iota(jnp.int32, pk.shape, 1)
    first = lane < QK_ROPE_DIM
    pe = jnp.where(first, pk, 0.0)
    ssq = jnp.sum(nope * nope, axis=-1, keepdims=True) + jnp.sum(pe * pe, axis=-1, keepdims=True)
    r = lax.rsqrt(ssq / QK_HEAD_DIM + EPS)
    a = pk * g_pk * tab
    rot = jnp.where(first, a + pltpu.roll(a, QK_ROPE_DIM, 1), 0.0)
    return nope * r * g_nope, rot * r


def _mla_kernel(q_ref, kn_ref, v_ref, kp_ref, tabk_ref, tabq_ref,
                gqn_ref, gqp_ref, gkn_ref, gkp_ref, o_ref,
                k_scr, m_scr, l_scr, acc_scr, *, blk, scale):
    d = QK_NOPE_DIM

    @pl.when(pl.program_id(2) == 0)
    def _():
        def prep(c, carry):
            rows = pl.ds(pl.multiple_of(c * blk, blk), blk)
            kn, kr = _mla_norm_rope(kn_ref[rows, :].astype(F32), kp_ref[rows, :].astype(F32),
                                    tabk_ref[rows, :], gkn_ref[...], gkp_ref[...])
            k_scr[rows, :d] = kn.astype(BF16)
            k_scr[rows, d:] = kr.astype(BF16)
            return carry
        lax.fori_loop(0, kn_ref.shape[0] // blk, prep, 0)

    qn, qr = _mla_norm_rope(q_ref[:, :d].astype(F32), q_ref[:, d:].astype(F32),
                            tabq_ref[...], gqn_ref[...], gqp_ref[...])
    qb = jnp.concatenate([(qn * scale).astype(BF16), (qr * scale).astype(BF16)], axis=-1)
    _softmax_attention(qb, k_scr, v_ref, None, o_ref, m_scr, l_scr, acc_scr, blk=blk)


def _mla_attention(q_ext, kv, down, tab, gq, gk, *, kp_col, blk=512):
    b, s, _ = q_ext.shape
    h_n = N_MLA_HEADS
    d = QK_NOPE_DIM
    blk = min(blk, s)

    def pack_gain(g):
        g_pe = g[d:]
        half = QK_ROPE_DIM // 2
        g_sw = jnp.concatenate([g_pe[half:], g_pe[:half]])
        return (g[:d].reshape(1, d).astype(F32),
                jnp.concatenate([g_pe, g_sw]).reshape(1, LANES).astype(F32))

    gqn, gqp = pack_gain(gq)
    gkn, gkp = pack_gain(gk)
    full = lambda col_fn: pl.BlockSpec((None, s, LANES), lambda bi, h, qi: (bi, 0, col_fn(h)))
    vec = pl.BlockSpec((1, LANES), lambda bi, h, qi: (0, 0))
    return pl.pallas_call(
        functools.partial(_mla_kernel, blk=blk, scale=QK_HEAD_DIM ** -0.5),
        grid=(b, h_n, s // blk),
        in_specs=[pl.BlockSpec((None, blk, 2 * LANES), lambda bi, h, qi: (bi, qi, h)),
                  full(lambda h: 2 * h), full(lambda h: 2 * h + 1),
                  full(lambda h: kp_col), full(lambda h: 0),
                  pl.BlockSpec((None, blk, LANES), lambda bi, h, qi: (bi, qi, 0)),
                  vec, vec, vec, vec],
        out_specs=pl.BlockSpec((None, blk, V_HEAD_DIM), lambda bi, h, qi: (bi, qi, h)),
        out_shape=jax.ShapeDtypeStruct((b, s, h_n * V_HEAD_DIM), BF16),
        scratch_shapes=[pltpu.VMEM((s, 2 * LANES), BF16)] + _attn_scratch(blk, V_HEAD_DIM),
        compiler_params=_params(3), name="mla_attention")(
            q_ext, kv, kv, down, tab, tab, gqn, gqp, gkn, gkp)


def _sb_kernel(q_ref, k_ref, v_ref, o_ref, r_scr, acc_scr, *, blk, scale):
    qi = pl.program_id(2)
    qb = (q_ref[...].astype(F32) * scale).astype(BF16)
    row = lax.broadcasted_iota(jnp.int32, (blk, blk), 0)
    col = lax.broadcasted_iota(jnp.int32, (blk, blk), 1)
    strict_suffix = jnp.where(row > col, 1.0, 0.0).astype(BF16)
    r_scr[...] = jnp.zeros(r_scr.shape, F32)
    acc_scr[...] = jnp.zeros(acc_scr.shape, F32)

    def chunk(c, masked):
        rows = pl.ds(pl.multiple_of(c * blk, blk), blk)
        z = lax.dot_general(qb, k_ref[rows, :], (((1,), (1,)), ((), ())),
                            preferred_element_type=F32)
        log_beta = _log_sigmoid(z)
        log_rest = log_beta - z
        if masked:
            valid = col < row
            log_rest = jnp.where(valid, log_rest, 0.0)
        hi = log_rest.astype(BF16)
        lo = (log_rest - hi.astype(F32)).astype(BF16)
        suffix = (jnp.dot(hi, strict_suffix, preferred_element_type=F32)
                  + jnp.dot(lo, strict_suffix, preferred_element_type=F32))
        r_prev = r_scr[...]
        w = jnp.exp(log_beta + suffix + r_prev)
        if masked:
            w = jnp.where(valid, w, 0.0)
        acc_scr[...] += jnp.dot(w.astype(BF16), v_ref[rows, :], preferred_element_type=F32)
        r_scr[...] = r_prev + suffix[:, 0:1] + log_rest[:, 0:1]

    chunk(qi, True)

    def body(i, carry):
        chunk(qi - 1 - i, False)
        return carry

    lax.fori_loop(0, qi, body, 0)
    o_ref[...] = acc_scr[...].astype(o_ref.dtype)


def _sb_attention(proj, *, n_heads, q_col, k_col, v_col, blk=256):
    b, s, _ = proj.shape
    d = HEAD_DIM
    blk = min(blk, s)
    kv_spec = lambda col: pl.BlockSpec((None, s, d), lambda bi, h, qi: (bi, 0, col + h))
    return pl.pallas_call(
        functools.partial(_sb_kernel, blk=blk, scale=d ** -0.5),
        grid=(b, n_heads, s // blk),
        in_specs=[pl.BlockSpec((None, blk, d), lambda bi, h, qi: (bi, qi, q_col + h)),
                  kv_spec(k_col), kv_spec(v_col)],
        out_specs=pl.BlockSpec((None, blk, d), lambda bi, h, qi: (bi, qi, h)),
        out_shape=jax.ShapeDtypeStruct((b, s, n_heads * d), BF16),
        scratch_shapes=[pltpu.VMEM((blk, 1), F32), pltpu.VMEM((blk, d), F32)],
        compiler_params=_params(3), name="sb_attention")(proj, proj, proj)


def _rope_kernel(ang_ref, cos_ref, sin_ref):
    ang = ang_ref[...]
    cos_ref[...] = jnp.cos(ang)
    sin_ref[...] = jnp.sin(ang)


def _rope_table(positions):
    b, s = positions.shape
    half = QK_ROPE_DIM // 2
    per_row = LANES // half
    inv_freq = ROPE_THETA ** (-jnp.arange(half, dtype=F32) / half)
    pos = jnp.repeat(positions.astype(F32), half, axis=-1).reshape(b * s // per_row, LANES)
    ang_in = pos * jnp.tile(inv_freq, per_row)[None, :]
    rows = ang_in.shape[0]
    spec = pl.BlockSpec((rows, LANES), lambda i: (0, 0))
    cos, sin = pl.pallas_call(
        _rope_kernel, grid=(1,), in_specs=[spec], out_specs=[spec, spec],
        out_shape=[jax.ShapeDtypeStruct((rows, LANES), F32)] * 2,
        compiler_params=_params(1), name="rope_table")(ang_in)
    cos = cos.reshape(b, s, half)
    sin = sin.reshape(b, s, half)
    return jnp.concatenate([cos, cos, -sin, sin], axis=-1)


def _mlp(x, g, w_up, w_down, name):
    a = _rms_matmul(x, g, w_up.astype(BF16), relu2=True, name=name + "_up")
    return _mm_res([a], [w_down.astype(BF16)], x, tk=512, name=name + "_down")


def _sb_fox_layer(x, batch, seq, g, w_in, b_f, fox_q_g, fox_k_g, w_o):
    d_model = x.shape[1]
    n_heads = d_model // (2 * HEAD_DIM)
    width = n_heads * HEAD_DIM
    w_main = w_in[:, :6 * width].astype(BF16)
    w_f = jnp.pad(w_in[:, 6 * width:], ((0, 0), (0, LANES - n_heads))).astype(BF16)
    proj, f_logit = _rms_matmul(x, g, w_main, w_aux=w_f, name="in_proj")
    b_pad = jnp.pad(b_f.astype(F32), (0, LANES - n_heads)).reshape(1, LANES)
    f_cum = _forget_cumsum(f_logit, b_pad, batch, seq, n_heads)
    proj = proj.reshape(batch, seq, 6 * width)
    o_sb = _sb_attention(proj, n_heads=n_heads, q_col=0, k_col=n_heads, v_col=2 * n_heads)
    o_fx = _fox_attention(proj, f_cum, fox_q_g, fox_k_g, n_heads=n_heads,
                          q_col=3 * n_heads, k_col=4 * n_heads, v_col=5 * n_heads)
    w_o = w_o.astype(BF16)
    return _mm_res([o_sb.reshape(-1, width), o_fx.reshape(-1, width)],
                   [w_o[:width], w_o[width:]], x, name="sf_out_proj")


def _swap_halves(w):
    half = w.shape[-1] // 2
    return jnp.concatenate([w[..., half:], w[..., :half]], axis=-1)


def _mla_layer(x, batch, seq, tab, g, w_down, q_a_g, kv_a_g, w_uq, w_ukv, q_g, k_g, w_o):
    lora = Q_LORA_RANK + KV_LORA_RANK
    w_pe = w_down[:, lora:]
    w_down_ext = jnp.concatenate([w_down, _swap_halves(w_pe)], axis=1).astype(BF16)
    down = _rms_matmul(x, g, w_down_ext, out_dtype=F32, tn=w_down_ext.shape[1], name="mla_down")
    w_uq_h = w_uq.reshape(Q_LORA_RANK, N_MLA_HEADS, QK_HEAD_DIM)
    w_uq_ext = jnp.concatenate([w_uq_h, _swap_halves(w_uq_h[..., QK_NOPE_DIM:])], axis=-1)
    w_uq_ext = w_uq_ext.reshape(Q_LORA_RANK, -1).astype(BF16)
    q_ext = _rms_matmul(down, q_a_g, w_uq_ext, xcol=0, name="mla_uq")
    kv = _rms_matmul(down, kv_a_g, w_ukv.astype(BF16), xcol=1, name="mla_ukv")
    o = _mla_attention(q_ext.reshape(batch, seq, -1), kv.reshape(batch, seq, -1),
                       down.reshape(batch, seq, -1), tab, q_g, k_g, kp_col=lora // LANES)
    return _mm_res([o.reshape(batch * seq, -1)], [w_o.astype(BF16)], x, name="mla_out_proj")


def kernel(x, positions, ln_mix_g, ln_mlp_g, sf_w_in, sf_b_f, fox_q_g, fox_k_g, sf_w_o,
           mla_w_down, mla_q_a_g, mla_kv_a_g, mla_w_uq, mla_w_ukv, mla_q_g, mla_k_g,
           mla_w_o, mlp_w_up, mlp_w_down):
    batch, seq, d_model = x.shape
    depth = ln_mix_g.shape[0]
    tab = _rope_table(positions)
    h = x.reshape(batch * seq, d_model)
    for layer in range(depth):
        i = layer // 2
        if layer % 2 == 0:
            h = _sb_fox_layer(h, batch, seq, ln_mix_g[layer], sf_w_in[i], sf_b_f[i],
                              fox_q_g[i], fox_k_g[i], sf_w_o[i])
        else:
            h = _mla_layer(h, batch, seq, tab, ln_mix_g[layer], mla_w_down[i], mla_q_a_g[i],
                           mla_kv_a_g[i], mla_w_uq[i], mla_w_ukv[i], mla_q_g[i], mla_k_g[i],
                           mla_w_o[i])
        h = _mlp(h, ln_mlp_g[layer], mlp_w_up[layer], mlp_w_down[layer], "mlp%d" % layer)
    return h.reshape(batch, seq, d_model)
```

```python
import functools

import jax
import jax.numpy as jnp
import numpy as np
from jax import lax
from jax.experimental import pallas as pl
from jax.experimental.pallas import tpu as pltpu

F32 = jnp.float32
BF16 = jnp.bfloat16

HEAD_DIM = 128
N_MLA_HEADS = 16
Q_LORA_RANK = 512
KV_LORA_RANK = 512
QK_NOPE_DIM = 128
QK_ROPE_DIM = 64
QK_HEAD_DIM = QK_NOPE_DIM + QK_ROPE_DIM
V_HEAD_DIM = 128
ROPE_THETA = 10000.0
EPS = 1e-6
LOG2E = 1.4426950408889634

LANES = 128
VMEM_LIMIT = 56 * 1024 * 1024
ARB = "arbitrary"


def _params(n_axes):
    return pltpu.CompilerParams(dimension_semantics=(ARB,) * n_axes,
                                vmem_limit_bytes=VMEM_LIMIT)


def _rms(x, g):
    ms = jnp.mean(x * x, axis=-1, keepdims=True)
    return x * lax.rsqrt(ms + EPS) * g


def _rms_matmul_kernel(x_ref, g_ref, w_ref, *rest, relu2, aux):
    if aux:
        wa_ref, o_ref, oa_ref, h_scr = rest
    else:
        o_ref, h_scr = rest

    @pl.when(pl.program_id(1) == 0)
    def _():
        h = _rms(x_ref[...].astype(F32), g_ref[...]).astype(BF16)
        h_scr[...] = h
        if aux:
            oa_ref[...] = jnp.dot(h, wa_ref[...], preferred_element_type=F32)

    acc = jnp.dot(h_scr[...], w_ref[...], preferred_element_type=F32)
    if relu2:
        acc = jnp.square(jnp.maximum(acc, 0.0))
    o_ref[...] = acc.astype(o_ref.dtype)


def _rms_matmul(x, g, w, *, xcol=0, out_dtype=BF16, relu2=False, w_aux=None,
                tm=1024, tn=1024, name):
    t = x.shape[0]
    k, n = w.shape
    tm, tn = min(tm, t), min(tn, n)
    assert t % tm == 0 and n % tn == 0
    aux = w_aux is not None
    in_specs = [pl.BlockSpec((tm, k), lambda i, j: (i, xcol)),
                pl.BlockSpec((1, k), lambda i, j: (0, 0)),
                pl.BlockSpec((k, tn), lambda i, j: (0, j))]
    out_specs = pl.BlockSpec((tm, tn), lambda i, j: (i, j))
    out_shape = jax.ShapeDtypeStruct((t, n), out_dtype)
    args = [x, g.reshape(1, k).astype(F32), w]
    if aux:
        na = w_aux.shape[1]
        in_specs.append(pl.BlockSpec((k, na), lambda i, j: (0, 0)))
        out_specs = [out_specs, pl.BlockSpec((tm, na), lambda i, j: (i, 0))]
        out_shape = [out_shape, jax.ShapeDtypeStruct((t, na), F32)]
        args.append(w_aux)
    return pl.pallas_call(
        functools.partial(_rms_matmul_kernel, relu2=relu2, aux=aux),
        grid=(t // tm, n // tn),
        in_specs=in_specs, out_specs=out_specs, out_shape=out_shape,
        scratch_shapes=[pltpu.VMEM((tm, k), BF16)],
        compiler_params=_params(2), name=name)(*args)


def _mm_res_kernel(*refs, n_pairs):
    a_refs = refs[:n_pairs]
    w_refs = refs[n_pairs:2 * n_pairs]
    r_ref, o_ref = refs[2 * n_pairs:]
    k = pl.program_id(2)

    @pl.when(k == 0)
    def _():
        o_ref[...] = r_ref[...]

    acc = jnp.dot(a_refs[0][...], w_refs[0][...], preferred_element_type=F32)
    for a_ref, w_ref in zip(a_refs[1:], w_refs[1:]):
        acc += jnp.dot(a_ref[...], w_ref[...], preferred_element_type=F32)
    o_ref[...] += acc


def _mm_res(a_list, w_list, r, *, tm=1024, tn=1024, tk=1024, name):
    t, n = r.shape
    k = a_list[0].shape[1]
    tm, tn, tk = min(tm, t), min(tn, n), min(tk, k)
    assert t % tm == 0 and n % tn == 0 and k % tk == 0
    n_pairs = len(a_list)
    in_specs = ([pl.BlockSpec((tm, tk), lambda i, j, kk: (i, kk))] * n_pairs
                + [pl.BlockSpec((tk, tn), lambda i, j, kk: (kk, j))] * n_pairs
                + [pl.BlockSpec((tm, tn), lambda i, j, kk: (i, j))])
    return pl.pallas_call(
        functools.partial(_mm_res_kernel, n_pairs=n_pairs),
        grid=(t // tm, n // tn, k // tk),
        in_specs=in_specs,
        out_specs=pl.BlockSpec((tm, tn), lambda i, j, kk: (i, j)),
        out_shape=jax.ShapeDtypeStruct((t, n), F32),
        compiler_params=_params(3), name=name)(*a_list, *w_list, r)


def _log_sigmoid(z):
    return jnp.minimum(z, 0.0) - jnp.log(1.0 + jnp.exp(-jnp.abs(z)))


def _split3(x):
    x1 = x.astype(BF16)
    r1 = x - x1.astype(F32)
    x2 = r1.astype(BF16)
    x3 = (r1 - x2.astype(F32)).astype(BF16)
    return x1, x2, x3


def _forget_cumsum_kernel(f_ref, b_ref, o_ref, *, n_heads, chunk):
    s = f_ref.shape[0]
    row = lax.broadcasted_iota(jnp.int32, (chunk, chunk), 0)
    col = lax.broadcasted_iota(jnp.int32, (chunk, chunk), 1)
    lower = jnp.where(col <= row, 1.0, 0.0).astype(BF16)
    lane = lax.broadcasted_iota(jnp.int32, (chunk, LANES), 1)

    def body(c, carry):
        rows = pl.ds(pl.multiple_of(c * chunk, chunk), chunk)
        lf = jnp.where(lane < n_heads, _log_sigmoid(f_ref[rows, :] + b_ref[...]), 0.0)
        cs = carry
        for part in _split3(lf):
            cs = cs + jnp.dot(lower, part, preferred_element_type=F32)
        hi, mid, lo = _split3(-LOG2E * cs)
        packed = (hi.astype(F32) + pltpu.roll(mid.astype(F32), n_heads, 1)
                  + pltpu.roll(lo.astype(F32), 2 * n_heads, 1))
        o_ref[rows, :] = packed.astype(BF16)
        return cs[chunk - 1:chunk, :]

    lax.fori_loop(0, s // chunk, body, jnp.zeros((1, LANES), F32))


def _forget_cumsum(f_logit, b_pad, seq, n_heads):
    t = f_logit.shape[0]
    return pl.pallas_call(
        functools.partial(_forget_cumsum_kernel, n_heads=n_heads, chunk=min(512, seq)),
        grid=(t // seq,),
        in_specs=[pl.BlockSpec((seq, LANES), lambda b: (b, 0)),
                  pl.BlockSpec((1, LANES), lambda b: (0, 0))],
        out_specs=pl.BlockSpec((seq, LANES), lambda b: (b, 0)),
        out_shape=jax.ShapeDtypeStruct((t, LANES), BF16),
        compiler_params=_params(1), name="forget_cumsum")(f_logit, b_pad)


def _softmax_attention(qb, k_scr, vt_scr, o_ref, s_a, s_b, m_scr, l_scr, acc_scr, *, bq, bk):
    assert bq == 2 * bk
    qi = pl.program_id(2)
    m_scr[...] = jnp.full(m_scr.shape, -jnp.inf, F32)
    l_scr[...] = jnp.zeros(l_scr.shape, F32)
    acc_scr[...] = jnp.zeros(acc_scr.shape, F32)
    n_full = 2 * qi

    def scores(c):
        rows = pl.ds(pl.multiple_of(c * bk, bk), bk)
        return lax.dot_general(k_scr[rows, :], qb, (((1,), (1,)), ((), ())),
                               preferred_element_type=F32)

    def consume(s_ref, c, masked):
        s = s_ref[...]
        if masked:
            key = c * bk + lax.broadcasted_iota(jnp.int32, s.shape, 0)
            query = qi * bq + lax.broadcasted_iota(jnp.int32, s.shape, 1)
            s = jnp.where(key <= query, s, -jnp.inf)
        m_prev = m_scr[...]
        m_new = jnp.maximum(m_prev, jnp.max(s, axis=0, keepdims=True))
        alpha = jnp.exp2(m_prev - m_new)
        p = jnp.exp2(s - m_new)
        l_scr[...] = alpha * l_scr[...] + jnp.sum(p, axis=0, keepdims=True)
        acc_scr[...] = alpha * acc_scr[...] + jnp.dot(
            vt_scr[c], p.astype(BF16), preferred_element_type=F32)
        m_scr[...] = m_new

    s_a[...] = scores(0)

    def pair(i, carry):
        c = 2 * i
        s_b[...] = scores(c + 1)
        consume(s_a, c, False)
        s_a[...] = scores(c + 2)
        consume(s_b, c + 1, False)
        return carry

    lax.fori_loop(0, qi, pair, 0)
    s_b[...] = scores(n_full + 1)
    consume(s_a, n_full, True)
    consume(s_b, n_full + 1, True)
    o_ref[...] = (acc_scr[...] * (1.0 / l_scr[...])).T.astype(o_ref.dtype)


def _attn_scratch(seq, bq, bk, dk, dv):
    return [pltpu.VMEM((seq, dk), BF16), pltpu.VMEM((seq // bk, dv, bk), BF16),
            pltpu.VMEM((bk, bq), F32), pltpu.VMEM((bk, bq), F32),
            pltpu.VMEM((1, bq), F32), pltpu.VMEM((1, bq), F32), pltpu.VMEM((dv, bq), F32)]


def _prep_values(v_ref, vt_scr, c, rows):
    vt_scr[c] = v_ref[rows, :].astype(F32).T.astype(BF16)


def _fox_kernel(q_ref, k_ref, v_ref, fa_ref, gq_ref, gk_ref, o_ref,
                k_scr, vt_scr, s_a, s_b, m_scr, l_scr, acc_scr, *, bq, bk, scale, n_heads):
    d = HEAD_DIM
    h = pl.program_id(1)

    @pl.when(pl.program_id(2) == 0)
    def _():
        def prep(c, carry):
            rows = pl.ds(pl.multiple_of(c * bk, bk), bk)
            k_scr[rows, :d] = _rms(k_ref[rows, :].astype(F32), gk_ref[...]).astype(BF16)
            k_scr[rows, d:] = fa_ref[rows, :]
            _prep_values(v_ref, vt_scr, c, rows)
            return carry
        lax.fori_loop(0, k_ref.shape[0] // bk, prep, 0)

    qn = _rms(q_ref[...].astype(F32), gq_ref[...]) * (scale * LOG2E)
    lane = lax.broadcasted_iota(jnp.int32, (bq, LANES), 1)
    pick = (lane == h) | (lane == h + n_heads) | (lane == h + 2 * n_heads)
    qb = jnp.concatenate([qn.astype(BF16), jnp.where(pick, 1.0, 0.0).astype(BF16)], axis=-1)
    _softmax_attention(qb, k_scr, vt_scr, o_ref, s_a, s_b, m_scr, l_scr, acc_scr, bq=bq, bk=bk)


def _fox_attention(proj, f_aug, gq, gk, *, n_heads, q_col, k_col, v_col, bq=512):
    b, s, _ = proj.shape
    d = HEAD_DIM
    bq = min(bq, s)
    bk = bq // 2
    kv_spec = lambda col: pl.BlockSpec((None, s, d), lambda bi, h, qi: (bi, 0, col + h))
    return pl.pallas_call(
        functools.partial(_fox_kernel, bq=bq, bk=bk, scale=d ** -0.5, n_heads=n_heads),
        grid=(b, n_heads, s // bq),
        in_specs=[pl.BlockSpec((None, bq, d), lambda bi, h, qi: (bi, qi, q_col + h)),
                  kv_spec(k_col), kv_spec(v_col),
                  pl.BlockSpec((None, s, LANES), lambda bi, h, qi: (bi, 0, 0)),
                  pl.BlockSpec((1, d), lambda bi, h, qi: (0, 0)),
                  pl.BlockSpec((1, d), lambda bi, h, qi: (0, 0))],
        out_specs=pl.BlockSpec((None, bq, d), lambda bi, h, qi: (bi, qi, h)),
        out_shape=jax.ShapeDtypeStruct((b, s, n_heads * d), BF16),
        scratch_shapes=_attn_scratch(s, bq, bk, 2 * LANES, d),
        compiler_params=_params(3), name="fox_attention")(
            proj, proj, proj, f_aug, gq.reshape(1, d).astype(F32), gk.reshape(1, d).astype(F32))


def _mla_norm_rope(nope, pk, tab, g_nope, g_pk):
    lane = lax.broadcasted_iota(jnp.int32, pk.shape, 1)
    first = lane < QK_ROPE_DIM
    pe = jnp.where(first, pk, 0.0)
    ssq = jnp.sum(nope * nope, axis=-1, keepdims=True) + jnp.sum(pe * pe, axis=-1, keepdims=True)
    r = lax.rsqrt(ssq / QK_HEAD_DIM + EPS)
    a = pk * g_pk * tab
    rot = jnp.where(first, a + pltpu.roll(a, QK_ROPE_DIM, 1), 0.0)
    return nope * r * g_nope, rot * r


def _mla_kernel(q_ref, kn_ref, v_ref, kp_ref, tabk_ref, tabq_ref,
                gqn_ref, gqp_ref, gkn_ref, gkp_ref, o_ref,
                k_scr, vt_scr, s_a, s_b, m_scr, l_scr, acc_scr, *, bq, bk, scale):
    d = QK_NOPE_DIM

    @pl.when(pl.program_id(2) == 0)
    def _():
        def prep(c, carry):
            rows = pl.ds(pl.multiple_of(c * bk, bk), bk)
            kn, kr = _mla_norm_rope(kn_ref[rows, :].astype(F32), kp_ref[rows, :].astype(F32),
                                    tabk_ref[rows, :], gkn_ref[...], gkp_ref[...])
            k_scr[rows, :d] = kn.astype(BF16)
            k_scr[rows, d:] = kr.astype(BF16)
            _prep_values(v_ref, vt_scr, c, rows)
            return carry
        lax.fori_loop(0, kn_ref.shape[0] // bk, prep, 0)

    qn, qr = _mla_norm_rope(q_ref[:, :d].astype(F32), q_ref[:, d:].astype(F32),
                            tabq_ref[...], gqn_ref[...], gqp_ref[...])
    c = scale * LOG2E
    qb = jnp.concatenate([(qn * c).astype(BF16), (qr * c).astype(BF16)], axis=-1)
    _softmax_attention(qb, k_scr, vt_scr, o_ref, s_a, s_b, m_scr, l_scr, acc_scr, bq=bq, bk=bk)


def _mla_attention(q_ext, kv, down, tab, gq, gk, *, kp_col, bq=512):
    b, s, _ = q_ext.shape
    h_n = N_MLA_HEADS
    d = QK_NOPE_DIM
    bq = min(bq, s)
    bk = bq // 2

    def pack_gain(g):
        g_pe = g[d:]
        half = QK_ROPE_DIM // 2
        g_sw = jnp.concatenate([g_pe[half:], g_pe[:half]])
        return (g[:d].reshape(1, d).astype(F32),
                jnp.concatenate([g_pe, g_sw]).reshape(1, LANES).astype(F32))

    gqn, gqp = pack_gain(gq)
    gkn, gkp = pack_gain(gk)
    full = lambda col_fn: pl.BlockSpec((None, s, LANES), lambda bi, h, qi: (bi, 0, col_fn(h)))
    vec = pl.BlockSpec((1, LANES), lambda bi, h, qi: (0, 0))
    return pl.pallas_call(
        functools.partial(_mla_kernel, bq=bq, bk=bk, scale=QK_HEAD_DIM ** -0.5),
        grid=(b, h_n, s // bq),
        in_specs=[pl.BlockSpec((None, bq, 2 * LANES), lambda bi, h, qi: (bi, qi, h)),
                  full(lambda h: 2 * h), full(lambda h: 2 * h + 1),
                  full(lambda h: kp_col), full(lambda h: 0),
                  pl.BlockSpec((None, bq, LANES), lambda bi, h, qi: (bi, qi, 0)),
                  vec, vec, vec, vec],
        out_specs=pl.BlockSpec((None, bq, V_HEAD_DIM), lambda bi, h, qi: (bi, qi, h)),
        out_shape=jax.ShapeDtypeStruct((b, s, h_n * V_HEAD_DIM), BF16),
        scratch_shapes=_attn_scratch(s, bq, bk, 2 * LANES, V_HEAD_DIM),
        compiler_params=_params(3), name="mla_attention")(
            q_ext, kv, kv, down, tab, tab, gqn, gqp, gkn, gkp)


def _sb_kernel(q_ref, k_ref, v_ref, o_ref, r_scr, acc_scr, *, blk, scale):
    qi = pl.program_id(2)
    qb = (q_ref[...].astype(F32) * scale).astype(BF16)
    row = lax.broadcasted_iota(jnp.int32, (blk, blk), 0)
    col = lax.broadcasted_iota(jnp.int32, (blk, blk), 1)
    strict_suffix = jnp.where(row > col, 1.0, 0.0).astype(BF16)
    r_scr[...] = jnp.zeros(r_scr.shape, F32)
    acc_scr[...] = jnp.zeros(acc_scr.shape, F32)

    def chunk(c, masked):
        rows = pl.ds(pl.multiple_of(c * blk, blk), blk)
        z = lax.dot_general(qb, k_ref[rows, :], (((1,), (1,)), ((), ())),
                            preferred_element_type=F32)
        log_beta = _log_sigmoid(z)
        log_rest = log_beta - z
        if masked:
            valid = col < row
            log_rest = jnp.where(valid, log_rest, 0.0)
        hi = log_rest.astype(BF16)
        lo = (log_rest - hi.astype(F32)).astype(BF16)
        suffix = (jnp.dot(hi, strict_suffix, preferred_element_type=F32)
                  + jnp.dot(lo, strict_suffix, preferred_element_type=F32))
        r_prev = r_scr[...]
        w = jnp.exp(log_beta + suffix + r_prev)
        if masked:
            w = jnp.where(valid, w, 0.0)
        acc_scr[...] += jnp.dot(w.astype(BF16), v_ref[rows, :], preferred_element_type=F32)
        r_scr[...] = r_prev + suffix[:, 0:1] + log_rest[:, 0:1]

    chunk(qi, True)

    def body(i, carry):
        chunk(qi - 1 - i, False)
        return carry

    lax.fori_loop(0, qi, body, 0)
    o_ref[...] = acc_scr[...].astype(o_ref.dtype)


def _sb_attention(proj, *, n_heads, q_col, k_col, v_col, blk=256):
    b, s, _ = proj.shape
    d = HEAD_DIM
    blk = min(blk, s)
    kv_spec = lambda col: pl.BlockSpec((None, s, d), lambda bi, h, qi: (bi, 0, col + h))
    return pl.pallas_call(
        functools.partial(_sb_kernel, blk=blk, scale=d ** -0.5),
        grid=(b, n_heads, s // blk),
        in_specs=[pl.BlockSpec((None, blk, d), lambda bi, h, qi: (bi, qi, q_col + h)),
                  kv_spec(k_col), kv_spec(v_col)],
        out_specs=pl.BlockSpec((None, blk, d), lambda bi, h, qi: (bi, qi, h)),
        out_shape=jax.ShapeDtypeStruct((b, s, n_heads * d), BF16),
        scratch_shapes=[pltpu.VMEM((blk, 1), F32), pltpu.VMEM((blk, d), F32)],
        compiler_params=_params(3), name="sb_attention")(proj, proj, proj)


def _rope_kernel(ang_ref, cos_ref, sin_ref):
    ang = ang_ref[...]
    cos_ref[...] = jnp.cos(ang)
    sin_ref[...] = jnp.sin(ang)


def _rope_table(positions):
    b, s = positions.shape
    half = QK_ROPE_DIM // 2
    per_row = LANES // half
    inv_freq = ROPE_THETA ** (-jnp.arange(half, dtype=F32) / half)
    pos = jnp.repeat(positions.astype(F32), half, axis=-1).reshape(b * s // per_row, LANES)
    ang_in = pos * jnp.tile(inv_freq, per_row)[None, :]
    rows = ang_in.shape[0]
    spec = pl.BlockSpec((rows, LANES), lambda i: (0, 0))
    cos, sin = pl.pallas_call(
        _rope_kernel, grid=(1,), in_specs=[spec], out_specs=[spec, spec],
        out_shape=[jax.ShapeDtypeStruct((rows, LANES), F32)] * 2,
        compiler_params=_params(1), name="rope_table")(ang_in)
    cos = cos.reshape(b, s, half)
    sin = sin.reshape(b, s, half)
    return jnp.concatenate([cos, cos, -sin, sin], axis=-1)


def _mlp(x, g, w_up, w_down, name):
    a = _rms_matmul(x, g, w_up.astype(BF16), relu2=True, name=name + "_up")
    return _mm_res([a], [w_down.astype(BF16)], x, tk=512, name=name + "_down")


def _sb_fox_layer(x, batch, seq, g, w_in, b_f, fox_q_g, fox_k_g, w_o):
    d_model = x.shape[1]
    n_heads = d_model // (2 * HEAD_DIM)
    width = n_heads * HEAD_DIM
    w_main = w_in[:, :6 * width].astype(BF16)
    w_f = jnp.pad(w_in[:, 6 * width:], ((0, 0), (0, LANES - n_heads))).astype(BF16)
    proj, f_logit = _rms_matmul(x, g, w_main, w_aux=w_f, name="in_proj")
    b_pad = jnp.pad(b_f.astype(F32), (0, LANES - n_heads)).reshape(1, LANES)
    f_aug = _forget_cumsum(f_logit, b_pad, seq, n_heads).reshape(batch, seq, LANES)
    proj = proj.reshape(batch, seq, 6 * width)
    o_sb = _sb_attention(proj, n_heads=n_heads, q_col=0, k_col=n_heads, v_col=2 * n_heads)
    o_fx = _fox_attention(proj, f_aug, fox_q_g, fox_k_g, n_heads=n_heads,
                          q_col=3 * n_heads, k_col=4 * n_heads, v_col=5 * n_heads)
    w_o = w_o.astype(BF16)
    return _mm_res([o_sb.reshape(-1, width), o_fx.reshape(-1, width)],
                   [w_o[:width], w_o[width:]], x, name="sf_out_proj")


def _swap_halves(w):
    half = w.shape[-1] // 2
    return jnp.concatenate([w[..., half:], w[..., :half]], axis=-1)


def _mla_layer(x, batch, seq, tab, g, w_down, q_a_g, kv_a_g, w_uq, w_ukv, q_g, k_g, w_o):
    lora = Q_LORA_RANK + KV_LORA_RANK
    w_pe = w_down[:, lora:]
    w_down_ext = jnp.concatenate([w_down, _swap_halves(w_pe)], axis=1).astype(BF16)
    down = _rms_matmul(x, g, w_down_ext, out_dtype=F32, tn=w_down_ext.shape[1], name="mla_down")
    w_uq_h = w_uq.reshape(Q_LORA_RANK, N_MLA_HEADS, QK_HEAD_DIM)
    w_uq_ext = jnp.concatenate([w_uq_h, _swap_halves(w_uq_h[..., QK_NOPE_DIM:])], axis=-1)
    w_uq_ext = w_uq_ext.reshape(Q_LORA_RANK, -1).astype(BF16)
    q_ext = _rms_matmul(down, q_a_g, w_uq_ext, xcol=0, name="mla_uq")
    kv = _rms_matmul(down, kv_a_g, w_ukv.astype(BF16), xcol=1, name="mla_ukv")
    o = _mla_attention(q_ext.reshape(batch, seq, -1), kv.reshape(batch, seq, -1),
                       down.reshape(batch, seq, -1), tab, q_g, k_g, kp_col=lora // LANES)
    return _mm_res([o.reshape(batch * seq, -1)], [w_o.astype(BF16)], x, name="mla_out_proj")


def kernel(x, positions, ln_mix_g, ln_mlp_g, sf_w_in, sf_b_f, fox_q_g, fox_k_g, sf_w_o,
           mla_w_down, mla_q_a_g, mla_kv_a_g, mla_w_uq, mla_w_ukv, mla_q_g, mla_k_g,
           mla_w_o, mlp_w_up, mlp_w_down):
    batch, seq, d_model = x.shape
    depth = ln_mix_g.shape[0]
    tab = _rope_table(positions)
    h = x.reshape(batch * seq, d_model)
    for layer in range(depth):
        i = layer // 2
        if layer % 2 == 0:
            h = _sb_fox_layer(h, batch, seq, ln_mix_g[layer], sf_w_in[i], sf_b_f[i],
                              fox_q_g[i], fox_k_g[i], sf_w_o[i])
        else:
            h = _mla_layer(h, batch, seq, tab, ln_mix_g[layer], mla_w_down[i], mla_q_a_g[i],
                           mla_kv_a_g[i], mla_w_uq[i], mla_w_ukv[i], mla_q_g[i], mla_k_g[i],
                           mla_w_o[i])
        h = _mlp(h, ln_mlp_g[layer], mlp_w_up[layer], mlp_w_down[layer], "mlp%d" % layer)
    return h.reshape(batch, seq, d_model)
```

```python
import functools

import jax
import jax.numpy as jnp
import numpy as np
from jax import lax
from jax.experimental import pallas as pl
from jax.experimental.pallas import tpu as pltpu

F32 = jnp.float32
BF16 = jnp.bfloat16

HEAD_DIM = 128
N_MLA_HEADS = 16
Q_LORA_RANK = 512
KV_LORA_RANK = 512
QK_NOPE_DIM = 128
QK_ROPE_DIM = 64
QK_HEAD_DIM = QK_NOPE_DIM + QK_ROPE_DIM
V_HEAD_DIM = 128
ROPE_THETA = 10000.0
EPS = 1e-6
LOG2E = 1.4426950408889634

LANES = 128
VMEM_LIMIT = 56 * 1024 * 1024
ARB = "arbitrary"


def _params(n_axes):
    return pltpu.CompilerParams(dimension_semantics=(ARB,) * n_axes,
                                vmem_limit_bytes=VMEM_LIMIT)


def _rms(x, g):
    ms = jnp.mean(x * x, axis=-1, keepdims=True)
    return x * lax.rsqrt(ms + EPS) * g


def _rms_matmul_kernel(x_ref, g_ref, w_ref, *rest, relu2, aux):
    if aux:
        wa_ref, o_ref, oa_ref, h_scr = rest
    else:
        o_ref, h_scr = rest

    @pl.when(pl.program_id(1) == 0)
    def _():
        h = _rms(x_ref[...].astype(F32), g_ref[...]).astype(BF16)
        h_scr[...] = h
        if aux:
            oa_ref[...] = jnp.dot(h, wa_ref[...], preferred_element_type=F32)

    acc = jnp.dot(h_scr[...], w_ref[...], preferred_element_type=F32)
    if relu2:
        acc = jnp.square(jnp.maximum(acc, 0.0))
    o_ref[...] = acc.astype(o_ref.dtype)


def _rms_matmul(x, g, w, *, xcol=0, out_dtype=BF16, relu2=False, w_aux=None,
                tm=1024, tn=1024, name):
    t = x.shape[0]
    k, n = w.shape
    tm, tn = min(tm, t), min(tn, n)
    assert t % tm == 0 and n % tn == 0
    aux = w_aux is not None
    in_specs = [pl.BlockSpec((tm, k), lambda i, j: (i, xcol)),
                pl.BlockSpec((1, k), lambda i, j: (0, 0)),
                pl.BlockSpec((k, tn), lambda i, j: (0, j))]
    out_specs = pl.BlockSpec((tm, tn), lambda i, j: (i, j))
    out_shape = jax.ShapeDtypeStruct((t, n), out_dtype)
    args = [x, g.reshape(1, k).astype(F32), w]
    if aux:
        na = w_aux.shape[1]
        in_specs.append(pl.BlockSpec((k, na), lambda i, j: (0, 0)))
        out_specs = [out_specs, pl.BlockSpec((tm, na), lambda i, j: (i, 0))]
        out_shape = [out_shape, jax.ShapeDtypeStruct((t, na), F32)]
        args.append(w_aux)
    return pl.pallas_call(
        functools.partial(_rms_matmul_kernel, relu2=relu2, aux=aux),
        grid=(t // tm, n // tn),
        in_specs=in_specs, out_specs=out_specs, out_shape=out_shape,
        scratch_shapes=[pltpu.VMEM((tm, k), BF16)],
        compiler_params=_params(2), name=name)(*args)


def _mm_res_kernel(*refs, n_pairs):
    a_refs = refs[:n_pairs]
    w_refs = refs[n_pairs:2 * n_pairs]
    r_ref, o_ref = refs[2 * n_pairs:]
    k = pl.program_id(2)

    @pl.when(k == 0)
    def _():
        o_ref[...] = r_ref[...]

    acc = jnp.dot(a_refs[0][...], w_refs[0][...], preferred_element_type=F32)
    for a_ref, w_ref in zip(a_refs[1:], w_refs[1:]):
        acc += jnp.dot(a_ref[...], w_ref[...], preferred_element_type=F32)
    o_ref[...] += acc


def _mm_res(a_list, w_list, r, *, tm=1024, tn=1024, tk=1024, name):
    t, n = r.shape
    k = a_list[0].shape[1]
    tm, tn, tk = min(tm, t), min(tn, n), min(tk, k)
    assert t % tm == 0 and n % tn == 0 and k % tk == 0
    n_pairs = len(a_list)
    in_specs = ([pl.BlockSpec((tm, tk), lambda i, j, kk: (i, kk))] * n_pairs
                + [pl.BlockSpec((tk, tn), lambda i, j, kk: (kk, j))] * n_pairs
                + [pl.BlockSpec((tm, tn), lambda i, j, kk: (i, j))])
    return pl.pallas_call(
        functools.partial(_mm_res_kernel, n_pairs=n_pairs),
        grid=(t // tm, n // tn, k // tk),
        in_specs=in_specs,
        out_specs=pl.BlockSpec((tm, tn), lambda i, j, kk: (i, j)),
        out_shape=jax.ShapeDtypeStruct((t, n), F32),
        compiler_params=_params(3), name=name)(*a_list, *w_list, r)


def _log_sigmoid(z):
    return jnp.minimum(z, 0.0) - jnp.log(1.0 + jnp.exp(-jnp.abs(z)))


def _split3(x):
    x1 = x.astype(BF16)
    r1 = x - x1.astype(F32)
    x2 = r1.astype(BF16)
    x3 = (r1 - x2.astype(F32)).astype(BF16)
    return x1, x2, x3


def _forget_cumsum_kernel(f_ref, b_ref, o_ref, *, n_heads, chunk):
    s = f_ref.shape[0]
    row = lax.broadcasted_iota(jnp.int32, (chunk, chunk), 0)
    col = lax.broadcasted_iota(jnp.int32, (chunk, chunk), 1)
    lower = jnp.where(col <= row, 1.0, 0.0).astype(BF16)
    lane = lax.broadcasted_iota(jnp.int32, (chunk, LANES), 1)

    def body(c, carry):
        rows = pl.ds(pl.multiple_of(c * chunk, chunk), chunk)
        lf = jnp.where(lane < n_heads, _log_sigmoid(f_ref[rows, :] + b_ref[...]), 0.0)
        cs = carry
        for part in _split3(lf):
            cs = cs + jnp.dot(lower, part, preferred_element_type=F32)
        hi, mid, lo = _split3(-LOG2E * cs)
        packed = (hi.astype(F32) + pltpu.roll(mid.astype(F32), n_heads, 1)
                  + pltpu.roll(lo.astype(F32), 2 * n_heads, 1))
        o_ref[rows, :] = packed.astype(BF16)
        return cs[chunk - 1:chunk, :]

    lax.fori_loop(0, s // chunk, body, jnp.zeros((1, LANES), F32))


def _forget_cumsum(f_logit, b_pad, seq, n_heads):
    t = f_logit.shape[0]
    return pl.pallas_call(
        functools.partial(_forget_cumsum_kernel, n_heads=n_heads, chunk=min(512, seq)),
        grid=(t // seq,),
        in_specs=[pl.BlockSpec((seq, LANES), lambda b: (b, 0)),
                  pl.BlockSpec((1, LANES), lambda b: (0, 0))],
        out_specs=pl.BlockSpec((seq, LANES), lambda b: (b, 0)),
        out_shape=jax.ShapeDtypeStruct((t, LANES), BF16),
        compiler_params=_params(1), name="forget_cumsum")(f_logit, b_pad)


def _softmax_attention(qb, k_scr, vt_scr, o_ref, s_a, s_b, m_scr, l_scr, acc_scr, *, bq, bk):
    nk = bq // bk
    assert bq == nk * bk and nk % 2 == 0
    qi = pl.program_id(2)
    m_scr[...] = jnp.full(m_scr.shape, -jnp.inf, F32)
    l_scr[...] = jnp.zeros(l_scr.shape, F32)
    acc_scr[...] = jnp.zeros(acc_scr.shape, F32)
    n_full = nk * qi
    bufs = (s_a, s_b)

    def scores(c, q0):
        rows = pl.ds(pl.multiple_of(c * bk, bk), bk)
        return lax.dot_general(k_scr[rows, :], qb[q0:, :], (((1,), (1,)), ((), ())),
                               preferred_element_type=F32)

    def consume(s_ref, c, q0, masked):
        s = s_ref[:, q0:]
        if masked:
            key = lax.broadcasted_iota(jnp.int32, s.shape, 0)
            query = lax.broadcasted_iota(jnp.int32, s.shape, 1)
            s = jnp.where(key <= query, s, -jnp.inf)
        m_prev = m_scr[:, q0:]
        m_new = jnp.maximum(m_prev, jnp.max(s, axis=0, keepdims=True))
        alpha = jnp.exp2(m_prev - m_new)
        p = jnp.exp2(s - m_new)
        l_scr[:, q0:] = alpha * l_scr[:, q0:] + jnp.sum(p, axis=0, keepdims=True)
        acc_scr[:, q0:] = alpha * acc_scr[:, q0:] + jnp.dot(
            vt_scr[c], p.astype(BF16), preferred_element_type=F32)
        m_scr[:, q0:] = m_new

    s_a[...] = scores(0, 0)

    def group(i, carry):
        c = nk * i
        for j in range(nk):
            bufs[(j + 1) % 2][...] = scores(c + j + 1, 0)
            consume(bufs[j % 2], c + j, 0, False)
        return carry

    lax.fori_loop(0, qi, group, 0)
    for j in range(nk):
        if j + 1 < nk:
            bufs[(j + 1) % 2][:, (j + 1) * bk:] = scores(n_full + j + 1, (j + 1) * bk)
        consume(bufs[j % 2], n_full + j, j * bk, True)
    o_ref[...] = (acc_scr[...] * (1.0 / l_scr[...])).T.astype(o_ref.dtype)


def _attn_scratch(seq, bq, bk, dk, dv):
    return [pltpu.VMEM((seq, dk), BF16), pltpu.VMEM((seq // bk, dv, bk), BF16),
            pltpu.VMEM((bk, bq), F32), pltpu.VMEM((bk, bq), F32),
            pltpu.VMEM((1, bq), F32), pltpu.VMEM((1, bq), F32), pltpu.VMEM((dv, bq), F32)]


def _lane_sumsq(*parts):
    sq = jnp.concatenate([p * p for p in parts], axis=-1)
    hi = sq.astype(BF16)
    lo = (sq - hi.astype(F32)).astype(BF16)
    ones = jnp.ones((sq.shape[-1], LANES), BF16)
    return (jnp.dot(hi, ones, preferred_element_type=F32)
            + jnp.dot(lo, ones, preferred_element_type=F32))


def _rms_mxu(x, g):
    return x * lax.rsqrt(_lane_sumsq(x) / x.shape[-1] + EPS) * g


def _prep_values(v_ref, vt_scr, c, rows):
    vt_scr[c] = v_ref[rows, :].astype(F32).T.astype(BF16)


def _fox_kernel(q_ref, k_ref, v_ref, fa_ref, gq_ref, gk_ref, o_ref,
                k_scr, vt_scr, s_a, s_b, m_scr, l_scr, acc_scr, *, bq, bk, scale, n_heads):
    d = HEAD_DIM
    h = pl.program_id(1)

    @pl.when(pl.program_id(2) == 0)
    def _():
        def prep(c, carry):
            rows = pl.ds(pl.multiple_of(c * bk, bk), bk)
            k_scr[rows, :d] = _rms_mxu(k_ref[rows, :].astype(F32), gk_ref[...]).astype(BF16)
            k_scr[rows, d:] = fa_ref[rows, :]
            _prep_values(v_ref, vt_scr, c, rows)
            return carry
        lax.fori_loop(0, k_ref.shape[0] // bk, prep, 0)

    qn = _rms_mxu(q_ref[...].astype(F32), gq_ref[...]) * (scale * LOG2E)
    lane = lax.broadcasted_iota(jnp.int32, (bq, LANES), 1)
    pick = (lane == h) | (lane == h + n_heads) | (lane == h + 2 * n_heads)
    qb = jnp.concatenate([qn.astype(BF16), jnp.where(pick, 1.0, 0.0).astype(BF16)], axis=-1)
    _softmax_attention(qb, k_scr, vt_scr, o_ref, s_a, s_b, m_scr, l_scr, acc_scr, bq=bq, bk=bk)


def _fox_attention(proj, f_aug, gq, gk, *, n_heads, q_col, k_col, v_col, bq=1024, bk=256):
    b, s, _ = proj.shape
    d = HEAD_DIM
    bq = min(bq, s)
    kv_spec = lambda col: pl.BlockSpec((None, s, d), lambda bi, h, qi: (bi, 0, col + h))
    return pl.pallas_call(
        functools.partial(_fox_kernel, bq=bq, bk=bk, scale=d ** -0.5, n_heads=n_heads),
        grid=(b, n_heads, s // bq),
        in_specs=[pl.BlockSpec((None, bq, d), lambda bi, h, qi: (bi, qi, q_col + h)),
                  kv_spec(k_col), kv_spec(v_col),
                  pl.BlockSpec((None, s, LANES), lambda bi, h, qi: (bi, 0, 0)),
                  pl.BlockSpec((1, d), lambda bi, h, qi: (0, 0)),
                  pl.BlockSpec((1, d), lambda bi, h, qi: (0, 0))],
        out_specs=pl.BlockSpec((None, bq, d), lambda bi, h, qi: (bi, qi, h)),
        out_shape=jax.ShapeDtypeStruct((b, s, n_heads * d), BF16),
        scratch_shapes=_attn_scratch(s, bq, bk, 2 * LANES, d),
        compiler_params=_params(3), name="fox_attention")(
            proj, proj, proj, f_aug, gq.reshape(1, d).astype(F32), gk.reshape(1, d).astype(F32))


def _mla_norm_rope(nope, pk, tab, g_nope, g_pk):
    lane = lax.broadcasted_iota(jnp.int32, pk.shape, 1)
    first = lane < QK_ROPE_DIM
    r = lax.rsqrt(_lane_sumsq(nope, jnp.where(first, pk, 0.0)) / QK_HEAD_DIM + EPS)
    a = pk * g_pk * tab
    rot = jnp.where(first, a + pltpu.roll(a, QK_ROPE_DIM, 1), 0.0)
    return nope * r * g_nope, rot * r


def _mla_kernel(q_ref, kn_ref, v_ref, kp_ref, tabk_ref, tabq_ref,
                gqn_ref, gqp_ref, gkn_ref, gkp_ref, o_ref,
                k_scr, vt_scr, s_a, s_b, m_scr, l_scr, acc_scr, *, bq, bk, scale):
    d = QK_NOPE_DIM

    @pl.when(pl.program_id(2) == 0)
    def _():
        def prep(c, carry):
            rows = pl.ds(pl.multiple_of(c * bk, bk), bk)
            kn, kr = _mla_norm_rope(kn_ref[rows, :].astype(F32), kp_ref[rows, :].astype(F32),
                                    tabk_ref[rows, :], gkn_ref[...], gkp_ref[...])
            k_scr[rows, :d] = kn.astype(BF16)
            k_scr[rows, d:] = kr.astype(BF16)
            _prep_values(v_ref, vt_scr, c, rows)
            return carry
        lax.fori_loop(0, kn_ref.shape[0] // bk, prep, 0)

    qn, qr = _mla_norm_rope(q_ref[:, :d].astype(F32), q_ref[:, d:].astype(F32),
                            tabq_ref[...], gqn_ref[...], gqp_ref[...])
    c = scale * LOG2E
    qb = jnp.concatenate([(qn * c).astype(BF16), (qr * c).astype(BF16)], axis=-1)
    _softmax_attention(qb, k_scr, vt_scr, o_ref, s_a, s_b, m_scr, l_scr, acc_scr, bq=bq, bk=bk)


def _mla_attention(q_ext, kv, down, tab, gq, gk, *, kp_col, bq=1024, bk=256):
    b, s, _ = q_ext.shape
    h_n = N_MLA_HEADS
    d = QK_NOPE_DIM
    bq = min(bq, s)

    def pack_gain(g):
        g_pe = g[d:]
        half = QK_ROPE_DIM // 2
        g_sw = jnp.concatenate([g_pe[half:], g_pe[:half]])
        return (g[:d].reshape(1, d).astype(F32),
                jnp.concatenate([g_pe, g_sw]).reshape(1, LANES).astype(F32))

    gqn, gqp = pack_gain(gq)
    gkn, gkp = pack_gain(gk)
    full = lambda col_fn: pl.BlockSpec((None, s, LANES), lambda bi, h, qi: (bi, 0, col_fn(h)))
    vec = pl.BlockSpec((1, LANES), lambda bi, h, qi: (0, 0))
    return pl.pallas_call(
        functools.partial(_mla_kernel, bq=bq, bk=bk, scale=QK_HEAD_DIM ** -0.5),
        grid=(b, h_n, s // bq),
        in_specs=[pl.BlockSpec((None, bq, 2 * LANES), lambda bi, h, qi: (bi, qi, h)),
                  full(lambda h: 2 * h), full(lambda h: 2 * h + 1),
                  full(lambda h: kp_col), full(lambda h: 0),
                  pl.BlockSpec((None, bq, LANES), lambda bi, h, qi: (bi, qi, 0)),
                  vec, vec, vec, vec],
        out_specs=pl.BlockSpec((None, bq, V_HEAD_DIM), lambda bi, h, qi: (bi, qi, h)),
        out_shape=jax.ShapeDtypeStruct((b, s, h_n * V_HEAD_DIM), BF16),
        scratch_shapes=_attn_scratch(s, bq, bk, 2 * LANES, V_HEAD_DIM),
        compiler_params=_params(3), name="mla_attention")(
            q_ext, kv, kv, down, tab, tab, gqn, gqp, gkn, gkp)


F32_EXP2_ZERO = -150.0


def _sb_kernel(q_ref, k_ref, v_ref, o_ref, vt_scr, r_scr, acc_scr, *, bq, bk, scale):
    assert bq == 2 * bk
    qi = pl.program_id(2)

    @pl.when(qi == 0)
    def _():
        def prep(c, carry):
            _prep_values(v_ref, vt_scr, c, pl.ds(pl.multiple_of(c * bk, bk), bk))
            return carry
        lax.fori_loop(0, k_ref.shape[0] // bk, prep, 0)

    qb = (q_ref[...].astype(F32) * (scale * LOG2E)).astype(BF16)
    row = lax.broadcasted_iota(jnp.int32, (bk, bk), 0)
    col = lax.broadcasted_iota(jnp.int32, (bk, bk), 1)
    later = jnp.where(col > row, 1.0, 0.0).astype(BF16)
    r_scr[...] = jnp.zeros(r_scr.shape, F32)
    acc_scr[...] = jnp.zeros(acc_scr.shape, F32)

    def chunk(c, masked):
        rows = pl.ds(pl.multiple_of(c * bk, bk), bk)
        z = lax.dot_general(k_ref[rows, :], qb, (((1,), (1,)), ((), ())),
                            preferred_element_type=F32)
        log_beta = jnp.minimum(z, 0.0) - jnp.log2(1.0 + jnp.exp2(-jnp.abs(z)))
        log_rest = log_beta - z
        if masked:
            key = c * bk + lax.broadcasted_iota(jnp.int32, z.shape, 0)
            query = qi * bq + lax.broadcasted_iota(jnp.int32, z.shape, 1)
            valid = key < query
            log_rest = jnp.where(valid, log_rest, 0.0)
        hi = log_rest.astype(BF16)
        lo = (log_rest - hi.astype(F32)).astype(BF16)
        suffix = (jnp.dot(later, hi, preferred_element_type=F32)
                  + jnp.dot(later, lo, preferred_element_type=F32))
        r_prev = r_scr[...]
        w = jnp.exp2(log_beta + suffix + r_prev)
        if masked:
            w = jnp.where(valid, w, 0.0)
        acc_scr[...] += jnp.dot(vt_scr[c], w.astype(BF16), preferred_element_type=F32)
        r_scr[...] = r_prev + suffix[0:1, :] + log_rest[0:1, :]

    chunk(2 * qi + 1, True)
    chunk(2 * qi, True)

    @pl.when(qi > 0)
    def _():
        chunk(2 * qi - 1, False)

    def more(c):
        return jnp.logical_and(c >= 0, jnp.max(r_scr[...]) > F32_EXP2_ZERO)

    def body(c):
        chunk(c, False)
        return c - 1

    lax.while_loop(more, body, 2 * qi - 2)
    o_ref[...] = acc_scr[...].T.astype(o_ref.dtype)


def _sb_attention(proj, *, n_heads, q_col, k_col, v_col, bq=512):
    b, s, _ = proj.shape
    d = HEAD_DIM
    bq = min(bq, s)
    bk = bq // 2
    kv_spec = lambda col: pl.BlockSpec((None, s, d), lambda bi, h, qi: (bi, 0, col + h))
    return pl.pallas_call(
        functools.partial(_sb_kernel, bq=bq, bk=bk, scale=d ** -0.5),
        grid=(b, n_heads, s // bq),
        in_specs=[pl.BlockSpec((None, bq, d), lambda bi, h, qi: (bi, qi, q_col + h)),
                  kv_spec(k_col), kv_spec(v_col)],
        out_specs=pl.BlockSpec((None, bq, d), lambda bi, h, qi: (bi, qi, h)),
        out_shape=jax.ShapeDtypeStruct((b, s, n_heads * d), BF16),
        scratch_shapes=[pltpu.VMEM((s // bk, d, bk), BF16), pltpu.VMEM((1, bq), F32),
                        pltpu.VMEM((d, bq), F32)],
        compiler_params=_params(3), name="sb_attention")(proj, proj, proj)


def _rope_kernel(ang_ref, cos_ref, sin_ref):
    ang = ang_ref[...]
    cos_ref[...] = jnp.cos(ang)
    sin_ref[...] = jnp.sin(ang)


def _rope_table(positions):
    b, s = positions.shape
    half = QK_ROPE_DIM // 2
    per_row = LANES // half
    inv_freq = ROPE_THETA ** (-jnp.arange(half, dtype=F32) / half)
    pos = jnp.repeat(positions.astype(F32), half, axis=-1).reshape(b * s // per_row, LANES)
    ang_in = pos * jnp.tile(inv_freq, per_row)[None, :]
    rows = ang_in.shape[0]
    spec = pl.BlockSpec((rows, LANES), lambda i: (0, 0))
    cos, sin = pl.pallas_call(
        _rope_kernel, grid=(1,), in_specs=[spec], out_specs=[spec, spec],
        out_shape=[jax.ShapeDtypeStruct((rows, LANES), F32)] * 2,
        compiler_params=_params(1), name="rope_table")(ang_in)
    cos = cos.reshape(b, s, half)
    sin = sin.reshape(b, s, half)
    return jnp.concatenate([cos, cos, -sin, sin], axis=-1)


def _mlp(x, g, w_up, w_down, name):
    a = _rms_matmul(x, g, w_up.astype(BF16), relu2=True, name=name + "_up")
    return _mm_res([a], [w_down.astype(BF16)], x, tk=2048, name=name + "_down")


def _sb_fox_layer(x, batch, seq, g, w_in, b_f, fox_q_g, fox_k_g, w_o):
    d_model = x.shape[1]
    n_heads = d_model // (2 * HEAD_DIM)
    width = n_heads * HEAD_DIM
    w_main = w_in[:, :6 * width].astype(BF16)
    w_f = jnp.pad(w_in[:, 6 * width:], ((0, 0), (0, LANES - n_heads))).astype(BF16)
    proj, f_logit = _rms_matmul(x, g, w_main, w_aux=w_f, name="in_proj")
    b_pad = jnp.pad(b_f.astype(F32), (0, LANES - n_heads)).reshape(1, LANES)
    f_aug = _forget_cumsum(f_logit, b_pad, seq, n_heads).reshape(batch, seq, LANES)
    proj = proj.reshape(batch, seq, 6 * width)
    o_sb = _sb_attention(proj, n_heads=n_heads, q_col=0, k_col=n_heads, v_col=2 * n_heads)
    o_fx = _fox_attention(proj, f_aug, fox_q_g, fox_k_g, n_heads=n_heads,
                          q_col=3 * n_heads, k_col=4 * n_heads, v_col=5 * n_heads)
    w_o = w_o.astype(BF16)
    return _mm_res([o_sb.reshape(-1, width), o_fx.reshape(-1, width)],
                   [w_o[:width], w_o[width:]], x, name="sf_out_proj")


def _swap_halves(w):
    half = w.shape[-1] // 2
    return jnp.concatenate([w[..., half:], w[..., :half]], axis=-1)


def _mla_layer(x, batch, seq, tab, g, w_down, q_a_g, kv_a_g, w_uq, w_ukv, q_g, k_g, w_o):
    lora = Q_LORA_RANK + KV_LORA_RANK
    w_pe = w_down[:, lora:]
    w_down_ext = jnp.concatenate([w_down, _swap_halves(w_pe)], axis=1).astype(BF16)
    down = _rms_matmul(x, g, w_down_ext, out_dtype=F32, tn=w_down_ext.shape[1], name="mla_down")
    w_uq_h = w_uq.reshape(Q_LORA_RANK, N_MLA_HEADS, QK_HEAD_DIM)
    w_uq_ext = jnp.concatenate([w_uq_h, _swap_halves(w_uq_h[..., QK_NOPE_DIM:])], axis=-1)
    w_uq_ext = w_uq_ext.reshape(Q_LORA_RANK, -1).astype(BF16)
    q_ext = _rms_matmul(down, q_a_g, w_uq_ext, xcol=0, name="mla_uq")
    kv = _rms_matmul(down, kv_a_g, w_ukv.astype(BF16), xcol=1, name="mla_ukv")
    o = _mla_attention(q_ext.reshape(batch, seq, -1), kv.reshape(batch, seq, -1),
                       down.reshape(batch, seq, -1), tab, q_g, k_g, kp_col=lora // LANES)
    return _mm_res([o.reshape(batch * seq, -1)], [w_o.astype(BF16)], x, tk=2048,
                   name="mla_out_proj")


def kernel(x, positions, ln_mix_g, ln_mlp_g, sf_w_in, sf_b_f, fox_q_g, fox_k_g, sf_w_o,
           mla_w_down, mla_q_a_g, mla_kv_a_g, mla_w_uq, mla_w_ukv, mla_q_g, mla_k_g,
           mla_w_o, mlp_w_up, mlp_w_down):
    batch, seq, d_model = x.shape
    depth = ln_mix_g.shape[0]
    tab = _rope_table(positions)
    h = x.reshape(batch * seq, d_model)
    for layer in range(depth):
        i = layer // 2
        if layer % 2 == 0:
            h = _sb_fox_layer(h, batch, seq, ln_mix_g[layer], sf_w_in[i], sf_b_f[i],
                              fox_q_g[i], fox_k_g[i], sf_w_o[i])
        else:
            h = _mla_layer(h, batch, seq, tab, ln_mix_g[layer], mla_w_down[i], mla_q_a_g[i],
                           mla_kv_a_g[i], mla_w_uq[i], mla_w_ukv[i], mla_q_g[i], mla_k_g[i],
                           mla_w_o[i])
        h = _mlp(h, ln_mlp_g[layer], mlp_w_up[layer], mlp_w_down[layer], "mlp%d" % layer)
    return h.reshape(batch, seq, d_model)
```

```python
import functools

import jax
import jax.numpy as jnp
import numpy as np
from jax import lax
from jax.experimental import pallas as pl
from jax.experimental.pallas import tpu as pltpu

F32 = jnp.float32
BF16 = jnp.bfloat16

HEAD_DIM = 128
N_MLA_HEADS = 16
Q_LORA_RANK = 512
KV_LORA_RANK = 512
QK_NOPE_DIM = 128
QK_ROPE_DIM = 64
QK_HEAD_DIM = QK_NOPE_DIM + QK_ROPE_DIM
V_HEAD_DIM = 128
ROPE_THETA = 10000.0
EPS = 1e-6
LOG2E = 1.4426950408889634

LANES = 128
VMEM_LIMIT = 56 * 1024 * 1024
ARB = "arbitrary"


def _params(n_axes):
    return pltpu.CompilerParams(dimension_semantics=(ARB,) * n_axes,
                                vmem_limit_bytes=VMEM_LIMIT)


def _rms(x, g):
    ms = jnp.mean(x * x, axis=-1, keepdims=True)
    return x * lax.rsqrt(ms + EPS) * g


def _rms_matmul_kernel(x_ref, g_ref, w_ref, *rest, relu2, aux):
    if aux:
        wa_ref, o_ref, oa_ref, h_scr = rest
    else:
        o_ref, h_scr = rest

    @pl.when(pl.program_id(1) == 0)
    def _():
        h = _rms(x_ref[...].astype(F32), g_ref[...]).astype(BF16)
        h_scr[...] = h
        if aux:
            oa_ref[...] = jnp.dot(h, wa_ref[...], preferred_element_type=F32)

    acc = jnp.dot(h_scr[...], w_ref[...].astype(BF16), preferred_element_type=F32)
    if relu2:
        acc = jnp.square(jnp.maximum(acc, 0.0))
    o_ref[...] = acc.astype(o_ref.dtype)


def _weight_spec(w, layer, block, index):
    if w.ndim == 2:
        return pl.BlockSpec(block, index)
    return pl.BlockSpec((None,) + block, lambda *ids: (layer,) + index(*ids))


def _rms_matmul(x, g, w, *, layer=None, n=None, xcol=0, out_dtype=BF16, relu2=False,
                w_aux=None, tm=1024, tn=1024, name):
    t = x.shape[0]
    k = w.shape[-2]
    n = w.shape[-1] if n is None else n
    tm, tn = min(tm, t), min(tn, n)
    assert t % tm == 0 and n % tn == 0
    aux = w_aux is not None
    in_specs = [pl.BlockSpec((tm, k), lambda i, j: (i, xcol)),
                pl.BlockSpec((1, k), lambda i, j: (0, 0)),
                _weight_spec(w, layer, (k, tn), lambda i, j: (0, j))]
    out_specs = pl.BlockSpec((tm, tn), lambda i, j: (i, j))
    out_shape = jax.ShapeDtypeStruct((t, n), out_dtype)
    args = [x, g.reshape(1, k).astype(F32), w]
    if aux:
        na = w_aux.shape[1]
        in_specs.append(pl.BlockSpec((k, na), lambda i, j: (0, 0)))
        out_specs = [out_specs, pl.BlockSpec((tm, na), lambda i, j: (i, 0))]
        out_shape = [out_shape, jax.ShapeDtypeStruct((t, na), F32)]
        args.append(w_aux)
    return pl.pallas_call(
        functools.partial(_rms_matmul_kernel, relu2=relu2, aux=aux),
        grid=(t // tm, n // tn),
        in_specs=in_specs, out_specs=out_specs, out_shape=out_shape,
        scratch_shapes=[pltpu.VMEM((tm, k), BF16)],
        compiler_params=_params(2), name=name)(*args)


def _mm_res_kernel(*refs, n_pairs):
    a_refs = refs[:n_pairs]
    w_refs = refs[n_pairs:2 * n_pairs]
    r_ref, o_ref = refs[2 * n_pairs:]
    k = pl.program_id(2)

    @pl.when(k == 0)
    def _():
        o_ref[...] = r_ref[...]

    acc = jnp.dot(a_refs[0][...], w_refs[0][...].astype(BF16), preferred_element_type=F32)
    for a_ref, w_ref in zip(a_refs[1:], w_refs[1:]):
        acc += jnp.dot(a_ref[...], w_ref[...].astype(BF16), preferred_element_type=F32)
    o_ref[...] += acc


def _mm_res(a_list, w, r, *, layer=None, tm=1024, tn=1024, tk=1024, name):
    t, n = r.shape
    k = a_list[0].shape[1]
    tm, tn, tk = min(tm, t), min(tn, n), min(tk, k)
    assert t % tm == 0 and n % tn == 0 and k % tk == 0
    n_pairs = len(a_list)
    assert w.shape[-2] == n_pairs * k
    k_blocks = k // tk
    in_specs = ([pl.BlockSpec((tm, tk), lambda i, j, kk: (i, kk))] * n_pairs
                + [_weight_spec(w, layer, (tk, tn),
                                lambda i, j, kk, p=p: (p * k_blocks + kk, j))
                   for p in range(n_pairs)]
                + [pl.BlockSpec((tm, tn), lambda i, j, kk: (i, j))])
    return pl.pallas_call(
        functools.partial(_mm_res_kernel, n_pairs=n_pairs),
        grid=(t // tm, n // tn, k // tk),
        in_specs=in_specs,
        out_specs=pl.BlockSpec((tm, tn), lambda i, j, kk: (i, j)),
        out_shape=jax.ShapeDtypeStruct((t, n), F32),
        compiler_params=_params(3), name=name)(*a_list, *([w] * n_pairs), r)


def _log_sigmoid(z):
    return jnp.minimum(z, 0.0) - jnp.log(1.0 + jnp.exp(-jnp.abs(z)))


def _split3(x):
    x1 = x.astype(BF16)
    r1 = x - x1.astype(F32)
    x2 = r1.astype(BF16)
    x3 = (r1 - x2.astype(F32)).astype(BF16)
    return x1, x2, x3


def _forget_cumsum_kernel(f_ref, b_ref, o_ref, *, n_heads, chunk):
    s = f_ref.shape[0]
    row = lax.broadcasted_iota(jnp.int32, (chunk, chunk), 0)
    col = lax.broadcasted_iota(jnp.int32, (chunk, chunk), 1)
    lower = jnp.where(col <= row, 1.0, 0.0).astype(BF16)
    lane = lax.broadcasted_iota(jnp.int32, (chunk, LANES), 1)

    def body(c, carry):
        rows = pl.ds(pl.multiple_of(c * chunk, chunk), chunk)
        lf = jnp.where(lane < n_heads, _log_sigmoid(f_ref[rows, :] + b_ref[...]), 0.0)
        cs = carry
        for part in _split3(lf):
            cs = cs + jnp.dot(lower, part, preferred_element_type=F32)
        hi, mid, lo = _split3(-LOG2E * cs)
        packed = (hi.astype(F32) + pltpu.roll(mid.astype(F32), n_heads, 1)
                  + pltpu.roll(lo.astype(F32), 2 * n_heads, 1))
        o_ref[rows, :] = packed.astype(BF16)
        return cs[chunk - 1:chunk, :]

    lax.fori_loop(0, s // chunk, body, jnp.zeros((1, LANES), F32))


def _forget_cumsum(f_logit, b_pad, seq, n_heads):
    t = f_logit.shape[0]
    return pl.pallas_call(
        functools.partial(_forget_cumsum_kernel, n_heads=n_heads, chunk=min(512, seq)),
        grid=(t // seq,),
        in_specs=[pl.BlockSpec((seq, LANES), lambda b: (b, 0)),
                  pl.BlockSpec((1, LANES), lambda b: (0, 0))],
        out_specs=pl.BlockSpec((seq, LANES), lambda b: (b, 0)),
        out_shape=jax.ShapeDtypeStruct((t, LANES), BF16),
        compiler_params=_params(1), name="forget_cumsum")(f_logit, b_pad)


def _softmax_attention(qb, k_scr, vt_scr, o_ref, s_a, s_b, m_scr, acc_scr, *, bq, bk):
    nk = bq // bk
    assert bq == nk * bk and nk % 2 == 0
    qi = pl.program_id(2)
    m_scr[...] = jnp.full(m_scr.shape, -jnp.inf, F32)
    acc_scr[...] = jnp.zeros(acc_scr.shape, F32)
    n_full = nk * qi
    bufs = (s_a, s_b)

    def scores(c, q0):
        rows = pl.ds(pl.multiple_of(c * bk, bk), bk)
        return lax.dot_general(k_scr[rows, :], qb[q0:, :], (((1,), (1,)), ((), ())),
                               preferred_element_type=F32)

    def consume(s_ref, c, q0, masked):
        s = s_ref[:, q0:]
        if masked:
            key = lax.broadcasted_iota(jnp.int32, s.shape, 0)
            query = lax.broadcasted_iota(jnp.int32, s.shape, 1)
            s = jnp.where(key <= query, s, -jnp.inf)
        m_prev = m_scr[:, q0:]
        m_new = jnp.maximum(m_prev, jnp.max(s, axis=0, keepdims=True))
        alpha = jnp.exp2(m_prev - m_new)
        p = jnp.exp2((s - m_new).astype(BF16))
        acc_scr[:, q0:] = alpha * acc_scr[:, q0:] + jnp.dot(
            vt_scr[c], p, preferred_element_type=F32)
        m_scr[:, q0:] = m_new

    s_a[...] = scores(0, 0)

    def group(i, carry):
        c = nk * i
        for j in range(nk):
            bufs[(j + 1) % 2][...] = scores(c + j + 1, 0)
            consume(bufs[j % 2], c + j, 0, False)
        return carry

    lax.fori_loop(0, qi, group, 0)
    for j in range(nk):
        if j + 1 < nk:
            bufs[(j + 1) % 2][:, (j + 1) * bk:] = scores(n_full + j + 1, (j + 1) * bk)
        consume(bufs[j % 2], n_full + j, j * bk, True)
    dv = o_ref.shape[-1]
    o_ref[...] = (acc_scr[:dv, :] * (1.0 / acc_scr[dv:dv + 1, :])).T.astype(o_ref.dtype)


SUM_ROWS = 16


def _attn_scratch(seq, bq, bk, dk, dv):
    return [pltpu.VMEM((seq, dk), BF16), pltpu.VMEM((seq // bk, dv + SUM_ROWS, bk), BF16),
            pltpu.VMEM((bk, bq), F32), pltpu.VMEM((bk, bq), F32),
            pltpu.VMEM((1, bq), F32), pltpu.VMEM((dv + SUM_ROWS, bq), F32)]


def _lane_sumsq(*parts):
    sq = jnp.concatenate([p * p for p in parts], axis=-1)
    hi = sq.astype(BF16)
    lo = (sq - hi.astype(F32)).astype(BF16)
    ones = jnp.ones((sq.shape[-1], LANES), BF16)
    return (jnp.dot(hi, ones, preferred_element_type=F32)
            + jnp.dot(lo, ones, preferred_element_type=F32))


def _rms_mxu(x, g):
    return x * lax.rsqrt(_lane_sumsq(x) / x.shape[-1] + EPS) * g


def _prep_values(v_ref, vt_scr, c, rows):
    dv = v_ref.shape[-1]
    vt_scr[c, :dv, :] = v_ref[rows, :].astype(F32).T.astype(BF16)
    if vt_scr.shape[1] > dv:
        vt_scr[c, dv:, :] = jnp.ones((vt_scr.shape[1] - dv, vt_scr.shape[2]), BF16)


def _fox_kernel(q_ref, k_ref, v_ref, fa_ref, gq_ref, gk_ref, o_ref,
                k_scr, vt_scr, s_a, s_b, m_scr, acc_scr, *, bq, bk, scale, n_heads):
    d = HEAD_DIM
    h = pl.program_id(1)

    @pl.when(pl.program_id(2) == 0)
    def _():
        def prep(c, carry):
            rows = pl.ds(pl.multiple_of(c * bk, bk), bk)
            k_scr[rows, :d] = _rms_mxu(k_ref[rows, :].astype(F32), gk_ref[...]).astype(BF16)
            k_scr[rows, d:] = fa_ref[rows, :]
            _prep_values(v_ref, vt_scr, c, rows)
            return carry
        lax.fori_loop(0, k_ref.shape[0] // bk, prep, 0)

    qn = _rms_mxu(q_ref[...].astype(F32), gq_ref[...]) * (scale * LOG2E)
    lane = lax.broadcasted_iota(jnp.int32, (bq, LANES), 1)
    pick = (lane == h) | (lane == h + n_heads) | (lane == h + 2 * n_heads)
    qb = jnp.concatenate([qn.astype(BF16), jnp.where(pick, 1.0, 0.0).astype(BF16)], axis=-1)
    _softmax_attention(qb, k_scr, vt_scr, o_ref, s_a, s_b, m_scr, acc_scr, bq=bq, bk=bk)


def _fox_attention(proj, f_aug, gq, gk, *, n_heads, q_col, k_col, v_col, bq=1024, bk=256):
    b, s, _ = proj.shape
    d = HEAD_DIM
    bq = min(bq, s)
    kv_spec = lambda col: pl.BlockSpec((None, s, d), lambda bi, h, qi: (bi, 0, col + h))
    return pl.pallas_call(
        functools.partial(_fox_kernel, bq=bq, bk=bk, scale=d ** -0.5, n_heads=n_heads),
        grid=(b, n_heads, s // bq),
        in_specs=[pl.BlockSpec((None, bq, d), lambda bi, h, qi: (bi, qi, q_col + h)),
                  kv_spec(k_col), kv_spec(v_col),
                  pl.BlockSpec((None, s, LANES), lambda bi, h, qi: (bi, 0, 0)),
                  pl.BlockSpec((1, d), lambda bi, h, qi: (0, 0)),
                  pl.BlockSpec((1, d), lambda bi, h, qi: (0, 0))],
        out_specs=pl.BlockSpec((None, bq, d), lambda bi, h, qi: (bi, qi, h)),
        out_shape=jax.ShapeDtypeStruct((b, s, n_heads * d), BF16),
        scratch_shapes=_attn_scratch(s, bq, bk, 2 * LANES, d),
        compiler_params=_params(3), name="fox_attention")(
            proj, proj, proj, f_aug, gq.reshape(1, d).astype(F32), gk.reshape(1, d).astype(F32))


def _mla_norm_rope(nope, pk, tab, g_nope, g_pk):
    lane = lax.broadcasted_iota(jnp.int32, pk.shape, 1)
    first = lane < QK_ROPE_DIM
    r = lax.rsqrt(_lane_sumsq(nope, jnp.where(first, pk, 0.0)) / QK_HEAD_DIM + EPS)
    a = pk * g_pk * tab
    rot = jnp.where(first, a + pltpu.roll(a, QK_ROPE_DIM, 1), 0.0)
    return nope * r * g_nope, rot * r


def _mla_kernel(q_ref, kn_ref, v_ref, kp_ref, tabk_ref, tabq_ref,
                gqn_ref, gqp_ref, gkn_ref, gkp_ref, o_ref,
                k_scr, vt_scr, s_a, s_b, m_scr, acc_scr, *, bq, bk, scale):
    d = QK_NOPE_DIM

    @pl.when(pl.program_id(2) == 0)
    def _():
        def prep(c, carry):
            rows = pl.ds(pl.multiple_of(c * bk, bk), bk)
            kn, kr = _mla_norm_rope(kn_ref[rows, :].astype(F32), kp_ref[rows, :].astype(F32),
                                    tabk_ref[rows, :], gkn_ref[...], gkp_ref[...])
            k_scr[rows, :d] = kn.astype(BF16)
            k_scr[rows, d:] = kr.astype(BF16)
            _prep_values(v_ref, vt_scr, c, rows)
            return carry
        lax.fori_loop(0, kn_ref.shape[0] // bk, prep, 0)

    qn, qr = _mla_norm_rope(q_ref[:, :d].astype(F32), q_ref[:, d:].astype(F32),
                            tabq_ref[...], gqn_ref[...], gqp_ref[...])
    c = scale * LOG2E
    qb = jnp.concatenate([(qn * c).astype(BF16), (qr * c).astype(BF16)], axis=-1)
    _softmax_attention(qb, k_scr, vt_scr, o_ref, s_a, s_b, m_scr, acc_scr, bq=bq, bk=bk)


def _mla_attention(q_ext, kv, down, tab, gq, gk, *, kp_col, bq=1024, bk=256):
    b, s, _ = q_ext.shape
    h_n = N_MLA_HEADS
    d = QK_NOPE_DIM
    bq = min(bq, s)

    def pack_gain(g):
        g_pe = g[d:]
        half = QK_ROPE_DIM // 2
        g_sw = jnp.concatenate([g_pe[half:], g_pe[:half]])
        return (g[:d].reshape(1, d).astype(F32),
                jnp.concatenate([g_pe, g_sw]).reshape(1, LANES).astype(F32))

    gqn, gqp = pack_gain(gq)
    gkn, gkp = pack_gain(gk)
    full = lambda col_fn: pl.BlockSpec((None, s, LANES), lambda bi, h, qi: (bi, 0, col_fn(h)))
    vec = pl.BlockSpec((1, LANES), lambda bi, h, qi: (0, 0))
    return pl.pallas_call(
        functools.partial(_mla_kernel, bq=bq, bk=bk, scale=QK_HEAD_DIM ** -0.5),
        grid=(b, h_n, s // bq),
        in_specs=[pl.BlockSpec((None, bq, 2 * LANES), lambda bi, h, qi: (bi, qi, h)),
                  full(lambda h: 2 * h), full(lambda h: 2 * h + 1),
                  full(lambda h: kp_col), full(lambda h: 0),
                  pl.BlockSpec((None, bq, LANES), lambda bi, h, qi: (bi, qi, 0)),
                  vec, vec, vec, vec],
        out_specs=pl.BlockSpec((None, bq, V_HEAD_DIM), lambda bi, h, qi: (bi, qi, h)),
        out_shape=jax.ShapeDtypeStruct((b, s, h_n * V_HEAD_DIM), BF16),
        scratch_shapes=_attn_scratch(s, bq, bk, 2 * LANES, V_HEAD_DIM),
        compiler_params=_params(3), name="mla_attention")(
            q_ext, kv, kv, down, tab, tab, gqn, gqp, gkn, gkp)


F32_EXP2_ZERO = -150.0


def _sb_kernel(q_ref, k_ref, v_ref, o_ref, vt_scr, r_scr, acc_scr, *, bq, bk, scale):
    assert bq == 2 * bk
    qi = pl.program_id(2)

    @pl.when(qi == 0)
    def _():
        def prep(c, carry):
            _prep_values(v_ref, vt_scr, c, pl.ds(pl.multiple_of(c * bk, bk), bk))
            return carry
        lax.fori_loop(0, k_ref.shape[0] // bk, prep, 0)

    qb = (q_ref[...].astype(F32) * (scale * LOG2E)).astype(BF16)
    row = lax.broadcasted_iota(jnp.int32, (bk, bk), 0)
    col = lax.broadcasted_iota(jnp.int32, (bk, bk), 1)
    later = jnp.where(col > row, 1.0, 0.0).astype(BF16)
    r_scr[...] = jnp.zeros(r_scr.shape, F32)
    acc_scr[...] = jnp.zeros(acc_scr.shape, F32)

    def chunk(c, masked):
        rows = pl.ds(pl.multiple_of(c * bk, bk), bk)
        z = lax.dot_general(k_ref[rows, :], qb, (((1,), (1,)), ((), ())),
                            preferred_element_type=F32)
        log_beta = jnp.minimum(z, 0.0) - jnp.log2(1.0 + jnp.exp2(-jnp.abs(z)))
        log_rest = log_beta - z
        if masked:
            key = c * bk + lax.broadcasted_iota(jnp.int32, z.shape, 0)
            query = qi * bq + lax.broadcasted_iota(jnp.int32, z.shape, 1)
            valid = key < query
            log_rest = jnp.where(valid, log_rest, 0.0)
        hi = log_rest.astype(BF16)
        lo = (log_rest - hi.astype(F32)).astype(BF16)
        suffix = (jnp.dot(later, hi, preferred_element_type=F32)
                  + jnp.dot(later, lo, preferred_element_type=F32))
        r_prev = r_scr[...]
        w = jnp.exp2(log_beta + suffix + r_prev)
        if masked:
            w = jnp.where(valid, w, 0.0)
        acc_scr[...] += jnp.dot(vt_scr[c], w.astype(BF16), preferred_element_type=F32)
        r_scr[...] = r_prev + suffix[0:1, :] + log_rest[0:1, :]

    chunk(2 * qi + 1, True)
    chunk(2 * qi, True)

    @pl.when(qi > 0)
    def _():
        chunk(2 * qi - 1, False)

    def more(c):
        return jnp.logical_and(c >= 0, jnp.max(r_scr[...]) > F32_EXP2_ZERO)

    def body(c):
        chunk(c, False)
        return c - 1

    lax.while_loop(more, body, 2 * qi - 2)
    o_ref[...] = acc_scr[...].T.astype(o_ref.dtype)


def _sb_attention(proj, *, n_heads, q_col, k_col, v_col, bq=512):
    b, s, _ = proj.shape
    d = HEAD_DIM
    bq = min(bq, s)
    bk = bq // 2
    kv_spec = lambda col: pl.BlockSpec((None, s, d), lambda bi, h, qi: (bi, 0, col + h))
    return pl.pallas_call(
        functools.partial(_sb_kernel, bq=bq, bk=bk, scale=d ** -0.5),
        grid=(b, n_heads, s // bq),
        in_specs=[pl.BlockSpec((None, bq, d), lambda bi, h, qi: (bi, qi, q_col + h)),
                  kv_spec(k_col), kv_spec(v_col)],
        out_specs=pl.BlockSpec((None, bq, d), lambda bi, h, qi: (bi, qi, h)),
        out_shape=jax.ShapeDtypeStruct((b, s, n_heads * d), BF16),
        scratch_shapes=[pltpu.VMEM((s // bk, d, bk), BF16), pltpu.VMEM((1, bq), F32),
                        pltpu.VMEM((d, bq), F32)],
        compiler_params=_params(3), name="sb_attention")(proj, proj, proj)


def _rope_kernel(ang_ref, cos_ref, sin_ref):
    ang = ang_ref[...]
    cos_ref[...] = jnp.cos(ang)
    sin_ref[...] = jnp.sin(ang)


def _rope_table(positions):
    b, s = positions.shape
    half = QK_ROPE_DIM // 2
    per_row = LANES // half
    inv_freq = ROPE_THETA ** (-jnp.arange(half, dtype=F32) / half)
    pos = jnp.repeat(positions.astype(F32), half, axis=-1).reshape(b * s // per_row, LANES)
    ang_in = pos * jnp.tile(inv_freq, per_row)[None, :]
    rows = ang_in.shape[0]
    spec = pl.BlockSpec((rows, LANES), lambda i: (0, 0))
    cos, sin = pl.pallas_call(
        _rope_kernel, grid=(1,), in_specs=[spec], out_specs=[spec, spec],
        out_shape=[jax.ShapeDtypeStruct((rows, LANES), F32)] * 2,
        compiler_params=_params(1), name="rope_table")(ang_in)
    cos = cos.reshape(b, s, half)
    sin = sin.reshape(b, s, half)
    return jnp.concatenate([cos, cos, -sin, sin], axis=-1)


def _mlp(x, g, w_up, w_down, layer):
    a = _rms_matmul(x, g, w_up, layer=layer, relu2=True, name="mlp%d_up" % layer)
    return _mm_res([a], w_down, x, layer=layer, tk=2048, name="mlp%d_down" % layer)


def _sb_fox_layer(x, batch, seq, g, w_in, i, b_f, fox_q_g, fox_k_g, w_o):
    d_model = x.shape[1]
    n_heads = d_model // (2 * HEAD_DIM)
    width = n_heads * HEAD_DIM
    w_f = jnp.pad(w_in[i, :, 6 * width:], ((0, 0), (0, LANES - n_heads))).astype(BF16)
    proj, f_logit = _rms_matmul(x, g, w_in, layer=i, n=6 * width, w_aux=w_f, name="in_proj")
    b_pad = jnp.pad(b_f.astype(F32), (0, LANES - n_heads)).reshape(1, LANES)
    f_aug = _forget_cumsum(f_logit, b_pad, seq, n_heads).reshape(batch, seq, LANES)
    proj = proj.reshape(batch, seq, 6 * width)
    o_sb = _sb_attention(proj, n_heads=n_heads, q_col=0, k_col=n_heads, v_col=2 * n_heads)
    o_fx = _fox_attention(proj, f_aug, fox_q_g, fox_k_g, n_heads=n_heads,
                          q_col=3 * n_heads, k_col=4 * n_heads, v_col=5 * n_heads)
    return _mm_res([o_sb.reshape(-1, width), o_fx.reshape(-1, width)], w_o, x, layer=i,
                   name="sf_out_proj")


def _swap_halves(w):
    half = w.shape[-1] // 2
    return jnp.concatenate([w[..., half:], w[..., :half]], axis=-1)


def _mla_layer(x, batch, seq, tab, g, w_down, i, q_a_g, kv_a_g, w_uq, w_ukv, q_g, k_g, w_o):
    lora = Q_LORA_RANK + KV_LORA_RANK
    w_pe = w_down[:, lora:]
    w_down_ext = jnp.concatenate([w_down, _swap_halves(w_pe)], axis=1).astype(BF16)
    down = _rms_matmul(x, g, w_down_ext, out_dtype=F32, tn=w_down_ext.shape[1], name="mla_down")
    w_uq_h = w_uq.reshape(Q_LORA_RANK, N_MLA_HEADS, QK_HEAD_DIM)
    w_uq_ext = jnp.concatenate([w_uq_h, _swap_halves(w_uq_h[..., QK_NOPE_DIM:])], axis=-1)
    w_uq_ext = w_uq_ext.reshape(Q_LORA_RANK, -1).astype(BF16)
    q_ext = _rms_matmul(down, q_a_g, w_uq_ext, xcol=0, name="mla_uq")
    kv = _rms_matmul(down, kv_a_g, w_ukv, layer=i, xcol=1, name="mla_ukv")
    o = _mla_attention(q_ext.reshape(batch, seq, -1), kv.reshape(batch, seq, -1),
                       down.reshape(batch, seq, -1), tab, q_g, k_g, kp_col=lora // LANES)
    return _mm_res([o.reshape(batch * seq, -1)], w_o, x, layer=i, tk=2048, name="mla_out_proj")


def kernel(x, positions, ln_mix_g, ln_mlp_g, sf_w_in, sf_b_f, fox_q_g, fox_k_g, sf_w_o,
           mla_w_down, mla_q_a_g, mla_kv_a_g, mla_w_uq, mla_w_ukv, mla_q_g, mla_k_g,
           mla_w_o, mlp_w_up, mlp_w_down):
    batch, seq, d_model = x.shape
    depth = ln_mix_g.shape[0]
    tab = _rope_table(positions)
    h = x.reshape(batch * seq, d_model)
    for layer in range(depth):
        i = layer // 2
        if layer % 2 == 0:
            h = _sb_fox_layer(h, batch, seq, ln_mix_g[layer], sf_w_in, i, sf_b_f[i],
                              fox_q_g[i], fox_k_g[i], sf_w_o)
        else:
            h = _mla_layer(h, batch, seq, tab, ln_mix_g[layer], mla_w_down[i], i, mla_q_a_g[i],
                           mla_kv_a_g[i], mla_w_uq[i], mla_w_ukv, mla_q_g[i], mla_k_g[i],
                           mla_w_o)
        h = _mlp(h, ln_mlp_g[layer], mlp_w_up, mlp_w_down, layer)
    return h.reshape(batch, seq, d_model)
```

```python
import functools

import jax
import jax.numpy as jnp
import numpy as np
from jax import lax
from jax.experimental import pallas as pl
from jax.experimental.pallas import tpu as pltpu

F32 = jnp.float32
BF16 = jnp.bfloat16

HEAD_DIM = 128
N_MLA_HEADS = 16
Q_LORA_RANK = 512
KV_LORA_RANK = 512
QK_NOPE_DIM = 128
QK_ROPE_DIM = 64
QK_HEAD_DIM = QK_NOPE_DIM + QK_ROPE_DIM
V_HEAD_DIM = 128
ROPE_THETA = 10000.0
EPS = 1e-6
LOG2E = 1.4426950408889634

LANES = 128
VMEM_LIMIT = 56 * 1024 * 1024
ARB = "arbitrary"


def _params(n_axes):
    return pltpu.CompilerParams(dimension_semantics=(ARB,) * n_axes,
                                vmem_limit_bytes=VMEM_LIMIT)


def _rms(x, g):
    ms = jnp.mean(x * x, axis=-1, keepdims=True)
    return x * lax.rsqrt(ms + EPS) * g


def _rms_matmul_kernel(x_ref, g_ref, w_ref, *rest, relu2, aux):
    if aux:
        wa_ref, o_ref, oa_ref, h_scr = rest
    else:
        o_ref, h_scr = rest

    @pl.when(pl.program_id(1) == 0)
    def _():
        h = _rms(x_ref[...].astype(F32), g_ref[...]).astype(BF16)
        h_scr[...] = h
        if aux:
            oa_ref[...] = jnp.dot(h, wa_ref[...], preferred_element_type=F32)

    acc = jnp.dot(h_scr[...], w_ref[...].astype(BF16), preferred_element_type=F32)
    if relu2:
        acc = jnp.square(jnp.maximum(acc, 0.0))
    o_ref[...] = acc.astype(o_ref.dtype)


def _weight_spec(w, layer, block, index):
    if w.ndim == 2:
        return pl.BlockSpec(block, index)
    return pl.BlockSpec((None,) + block, lambda *ids: (layer,) + index(*ids))


def _rms_matmul(x, g, w, *, layer=None, n=None, xcol=0, out_dtype=BF16, relu2=False,
                w_aux=None, tm=1024, tn=1024, name):
    t = x.shape[0]
    k = w.shape[-2]
    n = w.shape[-1] if n is None else n
    tm, tn = min(tm, t), min(tn, n)
    assert t % tm == 0 and n % tn == 0
    aux = w_aux is not None
    in_specs = [pl.BlockSpec((tm, k), lambda i, j: (i, xcol)),
                pl.BlockSpec((1, k), lambda i, j: (0, 0)),
                _weight_spec(w, layer, (k, tn), lambda i, j: (0, j))]
    out_specs = pl.BlockSpec((tm, tn), lambda i, j: (i, j))
    out_shape = jax.ShapeDtypeStruct((t, n), out_dtype)
    args = [x, g.reshape(1, k).astype(F32), w]
    if aux:
        na = w_aux.shape[1]
        in_specs.append(pl.BlockSpec((k, na), lambda i, j: (0, 0)))
        out_specs = [out_specs, pl.BlockSpec((tm, na), lambda i, j: (i, 0))]
        out_shape = [out_shape, jax.ShapeDtypeStruct((t, na), F32)]
        args.append(w_aux)
    return pl.pallas_call(
        functools.partial(_rms_matmul_kernel, relu2=relu2, aux=aux),
        grid=(t // tm, n // tn),
        in_specs=in_specs, out_specs=out_specs, out_shape=out_shape,
        scratch_shapes=[pltpu.VMEM((tm, k), BF16)],
        compiler_params=_params(2), name=name)(*args)


def _mm_res_kernel(*refs, n_pairs):
    a_refs = refs[:n_pairs]
    w_refs = refs[n_pairs:2 * n_pairs]
    r_ref, o_ref = refs[2 * n_pairs:]
    k = pl.program_id(2)

    @pl.when(k == 0)
    def _():
        o_ref[...] = r_ref[...]

    acc = jnp.dot(a_refs[0][...], w_refs[0][...].astype(BF16), preferred_element_type=F32)
    for a_ref, w_ref in zip(a_refs[1:], w_refs[1:]):
        acc += jnp.dot(a_ref[...], w_ref[...].astype(BF16), preferred_element_type=F32)
    o_ref[...] += acc


def _mm_res(a_list, w, r, *, layer=None, tm=1024, tn=1024, tk=1024, name):
    t, n = r.shape
    k = a_list[0].shape[1]
    tm, tn, tk = min(tm, t), min(tn, n), min(tk, k)
    assert t % tm == 0 and n % tn == 0 and k % tk == 0
    n_pairs = len(a_list)
    assert w.shape[-2] == n_pairs * k
    k_blocks = k // tk
    in_specs = ([pl.BlockSpec((tm, tk), lambda i, j, kk: (i, kk))] * n_pairs
                + [_weight_spec(w, layer, (tk, tn),
                                lambda i, j, kk, p=p: (p * k_blocks + kk, j))
                   for p in range(n_pairs)]
                + [pl.BlockSpec((tm, tn), lambda i, j, kk: (i, j))])
    return pl.pallas_call(
        functools.partial(_mm_res_kernel, n_pairs=n_pairs),
        grid=(t // tm, n // tn, k // tk),
        in_specs=in_specs,
        out_specs=pl.BlockSpec((tm, tn), lambda i, j, kk: (i, j)),
        out_shape=jax.ShapeDtypeStruct((t, n), F32),
        compiler_params=_params(3), name=name)(*a_list, *([w] * n_pairs), r)


def _log_sigmoid(z):
    return jnp.minimum(z, 0.0) - jnp.log(1.0 + jnp.exp(-jnp.abs(z)))


def _split3(x):
    x1 = x.astype(BF16)
    r1 = x - x1.astype(F32)
    x2 = r1.astype(BF16)
    x3 = (r1 - x2.astype(F32)).astype(BF16)
    return x1, x2, x3


def _forget_cumsum_kernel(f_ref, b_ref, o_ref, *, n_heads, chunk):
    s = f_ref.shape[0]
    row = lax.broadcasted_iota(jnp.int32, (chunk, chunk), 0)
    col = lax.broadcasted_iota(jnp.int32, (chunk, chunk), 1)
    lower = jnp.where(col <= row, 1.0, 0.0).astype(BF16)
    lane = lax.broadcasted_iota(jnp.int32, (chunk, LANES), 1)

    def body(c, carry):
        rows = pl.ds(pl.multiple_of(c * chunk, chunk), chunk)
        lf = jnp.where(lane < n_heads, _log_sigmoid(f_ref[rows, :] + b_ref[...]), 0.0)
        cs = carry
        for part in _split3(lf):
            cs = cs + jnp.dot(lower, part, preferred_element_type=F32)
        hi, mid, lo = _split3(-LOG2E * cs)
        packed = (hi.astype(F32) + pltpu.roll(mid.astype(F32), n_heads, 1)
                  + pltpu.roll(lo.astype(F32), 2 * n_heads, 1))
        o_ref[rows, :] = packed.astype(BF16)
        return cs[chunk - 1:chunk, :]

    lax.fori_loop(0, s // chunk, body, jnp.zeros((1, LANES), F32))


def _forget_cumsum(f_logit, b_pad, seq, n_heads):
    t = f_logit.shape[0]
    return pl.pallas_call(
        functools.partial(_forget_cumsum_kernel, n_heads=n_heads, chunk=min(512, seq)),
        grid=(t // seq,),
        in_specs=[pl.BlockSpec((seq, LANES), lambda b: (b, 0)),
                  pl.BlockSpec((1, LANES), lambda b: (0, 0))],
        out_specs=pl.BlockSpec((seq, LANES), lambda b: (b, 0)),
        out_shape=jax.ShapeDtypeStruct((t, LANES), BF16),
        compiler_params=_params(1), name="forget_cumsum")(f_logit, b_pad)


def _softmax_attention(qb, k_scr, vt_scr, o_ref, s_a, s_b, m_scr, l_scr, acc_scr, *, bq, bk):
    nk = bq // bk
    assert bq == nk * bk and nk % 2 == 0
    qi = pl.program_id(2)
    m_scr[...] = jnp.full(m_scr.shape, -jnp.inf, F32)
    l_scr[...] = jnp.zeros(l_scr.shape, F32)
    acc_scr[...] = jnp.zeros(acc_scr.shape, F32)
    n_full = nk * qi
    bufs = (s_a, s_b)

    def scores(c, q0):
        rows = pl.ds(pl.multiple_of(c * bk, bk), bk)
        return lax.dot_general(k_scr[rows, :], qb[q0:, :], (((1,), (1,)), ((), ())),
                               preferred_element_type=F32)

    def consume(s_ref, c, q0, masked):
        s = s_ref[:, q0:]
        if masked:
            key = lax.broadcasted_iota(jnp.int32, s.shape, 0)
            query = lax.broadcasted_iota(jnp.int32, s.shape, 1)
            s = jnp.where(key <= query, s, -jnp.inf)
        m_prev = m_scr[:, q0:]
        m_new = jnp.maximum(m_prev, jnp.max(s, axis=0, keepdims=True))
        alpha = jnp.exp2(m_prev - m_new)
        p = jnp.exp2(s - m_new)
        l_scr[:, q0:] = alpha * l_scr[:, q0:] + jnp.sum(p, axis=0, keepdims=True)
        acc_scr[:, q0:] = alpha * acc_scr[:, q0:] + jnp.dot(
            vt_scr[c], p.astype(BF16), preferred_element_type=F32)
        m_scr[:, q0:] = m_new

    s_a[...] = scores(0, 0)

    def group(i, carry):
        c = nk * i
        for j in range(nk):
            bufs[(j + 1) % 2][...] = scores(c + j + 1, 0)
            consume(bufs[j % 2], c + j, 0, False)
        return carry

    lax.fori_loop(0, qi, group, 0)
    for j in range(nk):
        if j + 1 < nk:
            bufs[(j + 1) % 2][:, (j + 1) * bk:] = scores(n_full + j + 1, (j + 1) * bk)
        consume(bufs[j % 2], n_full + j, j * bk, True)
    o_ref[...] = (acc_scr[...] * (1.0 / l_scr[...])).T.astype(o_ref.dtype)


def _attn_scratch(seq, bq, bk, dk, dv):
    return [pltpu.VMEM((seq, dk), BF16), pltpu.VMEM((seq // bk, dv, bk), BF16),
            pltpu.VMEM((bk, bq), F32), pltpu.VMEM((bk, bq), F32),
            pltpu.VMEM((1, bq), F32), pltpu.VMEM((1, bq), F32), pltpu.VMEM((dv, bq), F32)]


def _lane_sumsq(*parts):
    sq = jnp.concatenate([p * p for p in parts], axis=-1)
    hi = sq.astype(BF16)
    lo = (sq - hi.astype(F32)).astype(BF16)
    ones = jnp.ones((sq.shape[-1], LANES), BF16)
    return (jnp.dot(hi, ones, preferred_element_type=F32)
            + jnp.dot(lo, ones, preferred_element_type=F32))


def _rms_mxu(x, g):
    return x * lax.rsqrt(_lane_sumsq(x) / x.shape[-1] + EPS) * g


def _prep_values(v_ref, vt_scr, c, rows):
    vt_scr[c] = v_ref[rows, :].astype(F32).T.astype(BF16)


def _fox_kernel(q_ref, k_ref, v_ref, fa_ref, gq_ref, gk_ref, o_ref,
                k_scr, vt_scr, s_a, s_b, m_scr, l_scr, acc_scr, *, bq, bk, scale, n_heads):
    d = HEAD_DIM
    h = pl.program_id(1)

    @pl.when(pl.program_id(2) == 0)
    def _():
        def prep(c, carry):
            rows = pl.ds(pl.multiple_of(c * bk, bk), bk)
            k_scr[rows, :d] = _rms_mxu(k_ref[rows, :].astype(F32), gk_ref[...]).astype(BF16)
            k_scr[rows, d:] = fa_ref[rows, :]
            _prep_values(v_ref, vt_scr, c, rows)
            return carry
        lax.fori_loop(0, k_ref.shape[0] // bk, prep, 0)

    qn = _rms_mxu(q_ref[...].astype(F32), gq_ref[...]) * (scale * LOG2E)
    lane = lax.broadcasted_iota(jnp.int32, (bq, LANES), 1)
    pick = (lane == h) | (lane == h + n_heads) | (lane == h + 2 * n_heads)
    qb = jnp.concatenate([qn.astype(BF16), jnp.where(pick, 1.0, 0.0).astype(BF16)], axis=-1)
    _softmax_attention(qb, k_scr, vt_scr, o_ref, s_a, s_b, m_scr, l_scr, acc_scr, bq=bq, bk=bk)


def _fox_attention(proj, f_aug, gq, gk, *, n_heads, q_col, k_col, v_col, bq=1024, bk=256):
    b, s, _ = proj.shape
    d = HEAD_DIM
    bq = min(bq, s)
    kv_spec = lambda col: pl.BlockSpec((None, s, d), lambda bi, h, qi: (bi, 0, col + h))
    return pl.pallas_call(
        functools.partial(_fox_kernel, bq=bq, bk=bk, scale=d ** -0.5, n_heads=n_heads),
        grid=(b, n_heads, s // bq),
        in_specs=[pl.BlockSpec((None, bq, d), lambda bi, h, qi: (bi, qi, q_col + h)),
                  kv_spec(k_col), kv_spec(v_col),
                  pl.BlockSpec((None, s, LANES), lambda bi, h, qi: (bi, 0, 0)),
                  pl.BlockSpec((1, d), lambda bi, h, qi: (0, 0)),
                  pl.BlockSpec((1, d), lambda bi, h, qi: (0, 0))],
        out_specs=pl.BlockSpec((None, bq, d), lambda bi, h, qi: (bi, qi, h)),
        out_shape=jax.ShapeDtypeStruct((b, s, n_heads * d), BF16),
        scratch_shapes=_attn_scratch(s, bq, bk, 2 * LANES, d),
        compiler_params=_params(3), name="fox_attention")(
            proj, proj, proj, f_aug, gq.reshape(1, d).astype(F32), gk.reshape(1, d).astype(F32))


def _mla_norm_rope(nope, pk, tab, g_nope, g_pk):
    lane = lax.broadcasted_iota(jnp.int32, pk.shape, 1)
    first = lane < QK_ROPE_DIM
    r = lax.rsqrt(_lane_sumsq(nope, jnp.where(first, pk, 0.0)) / QK_HEAD_DIM + EPS)
    a = pk * g_pk * tab
    rot = jnp.where(first, a + pltpu.roll(a, QK_ROPE_DIM, 1), 0.0)
    return nope * r * g_nope, rot * r


def _mla_kernel(q_ref, kn_ref, v_ref, kp_ref, tabk_ref, tabq_ref,
                gqn_ref, gqp_ref, gkn_ref, gkp_ref, o_ref,
                k_scr, vt_scr, s_a, s_b, m_scr, l_scr, acc_scr, *, bq, bk, scale):
    d = QK_NOPE_DIM

    @pl.when(pl.program_id(2) == 0)
    def _():
        def prep(c, carry):
            rows = pl.ds(pl.multiple_of(c * bk, bk), bk)
            kn, kr = _mla_norm_rope(kn_ref[rows, :].astype(F32), kp_ref[rows, :].astype(F32),
                                    tabk_ref[rows, :], gkn_ref[...], gkp_ref[...])
            k_scr[rows, :d] = kn.astype(BF16)
            k_scr[rows, d:] = kr.astype(BF16)
            _prep_values(v_ref, vt_scr, c, rows)
            return carry
        lax.fori_loop(0, kn_ref.shape[0] // bk, prep, 0)

    qn, qr = _mla_norm_rope(q_ref[:, :d].astype(F32), q_ref[:, d:].astype(F32),
                            tabq_ref[...], gqn_ref[...], gqp_ref[...])
    c = scale * LOG2E
    qb = jnp.concatenate([(qn * c).astype(BF16), (qr * c).astype(BF16)], axis=-1)
    _softmax_attention(qb, k_scr, vt_scr, o_ref, s_a, s_b, m_scr, l_scr, acc_scr, bq=bq, bk=bk)


def _mla_attention(q_ext, kv, down, tab, gq, gk, *, kp_col, bq=1024, bk=256):
    b, s, _ = q_ext.shape
    h_n = N_MLA_HEADS
    d = QK_NOPE_DIM
    bq = min(bq, s)

    def pack_gain(g):
        g_pe = g[d:]
        half = QK_ROPE_DIM // 2
        g_sw = jnp.concatenate([g_pe[half:], g_pe[:half]])
        return (g[:d].reshape(1, d).astype(F32),
                jnp.concatenate([g_pe, g_sw]).reshape(1, LANES).astype(F32))

    gqn, gqp = pack_gain(gq)
    gkn, gkp = pack_gain(gk)
    full = lambda col_fn: pl.BlockSpec((None, s, LANES), lambda bi, h, qi: (bi, 0, col_fn(h)))
    vec = pl.BlockSpec((1, LANES), lambda bi, h, qi: (0, 0))
    return pl.pallas_call(
        functools.partial(_mla_kernel, bq=bq, bk=bk, scale=QK_HEAD_DIM ** -0.5),
        grid=(b, h_n, s // bq),
        in_specs=[pl.BlockSpec((None, bq, 2 * LANES), lambda bi, h, qi: (bi, qi, h)),
                  full(lambda h: 2 * h), full(lambda h: 2 * h + 1),
                  full(lambda h: kp_col), full(lambda h: 0),
                  pl.BlockSpec((None, bq, LANES), lambda bi, h, qi: (bi, qi, 0)),
                  vec, vec, vec, vec],
        out_specs=pl.BlockSpec((None, bq, V_HEAD_DIM), lambda bi, h, qi: (bi, qi, h)),
        out_shape=jax.ShapeDtypeStruct((b, s, h_n * V_HEAD_DIM), BF16),
        scratch_shapes=_attn_scratch(s, bq, bk, 2 * LANES, V_HEAD_DIM),
        compiler_params=_params(3), name="mla_attention")(
            q_ext, kv, kv, down, tab, tab, gqn, gqp, gkn, gkp)


F32_EXP2_ZERO = -150.0


def _sb_kernel(q_ref, k_ref, v_ref, o_ref, vt_scr, r_scr, acc_scr, *, bq, bk, scale):
    assert bq == 2 * bk
    qi = pl.program_id(2)

    @pl.when(qi == 0)
    def _():
        def prep(c, carry):
            _prep_values(v_ref, vt_scr, c, pl.ds(pl.multiple_of(c * bk, bk), bk))
            return carry
        lax.fori_loop(0, k_ref.shape[0] // bk, prep, 0)

    qb = (q_ref[...].astype(F32) * (scale * LOG2E)).astype(BF16)
    row = lax.broadcasted_iota(jnp.int32, (bk, bk), 0)
    col = lax.broadcasted_iota(jnp.int32, (bk, bk), 1)
    later = jnp.where(col > row, 1.0, 0.0).astype(BF16)
    r_scr[...] = jnp.zeros(r_scr.shape, F32)
    acc_scr[...] = jnp.zeros(acc_scr.shape, F32)

    def chunk(c, masked):
        rows = pl.ds(pl.multiple_of(c * bk, bk), bk)
        z = lax.dot_general(k_ref[rows, :], qb, (((1,), (1,)), ((), ())),
                            preferred_element_type=F32)
        log_beta = jnp.minimum(z, 0.0) - jnp.log2(1.0 + jnp.exp2(-jnp.abs(z)))
        log_rest = log_beta - z
        if masked:
            key = c * bk + lax.broadcasted_iota(jnp.int32, z.shape, 0)
            query = qi * bq + lax.broadcasted_iota(jnp.int32, z.shape, 1)
            valid = key < query
            log_rest = jnp.where(valid, log_rest, 0.0)
        hi = log_rest.astype(BF16)
        lo = (log_rest - hi.astype(F32)).astype(BF16)
        suffix = (jnp.dot(later, hi, preferred_element_type=F32)
                  + jnp.dot(later, lo, preferred_element_type=F32))
        r_prev = r_scr[...]
        w = jnp.exp2(log_beta + suffix + r_prev)
        if masked:
            w = jnp.where(valid, w, 0.0)
        acc_scr[...] += jnp.dot(vt_scr[c], w.astype(BF16), preferred_element_type=F32)
        r_scr[...] = r_prev + suffix[0:1, :] + log_rest[0:1, :]

    chunk(2 * qi + 1, True)
    chunk(2 * qi, True)

    @pl.when(qi > 0)
    def _():
        chunk(2 * qi - 1, False)

    def more(c):
        return jnp.logical_and(c >= 0, jnp.max(r_scr[...]) > F32_EXP2_ZERO)

    def body(c):
        chunk(c, False)
        return c - 1

    lax.while_loop(more, body, 2 * qi - 2)
    o_ref[...] = acc_scr[...].T.astype(o_ref.dtype)


def _sb_attention(proj, *, n_heads, q_col, k_col, v_col, bq=512):
    b, s, _ = proj.shape
    d = HEAD_DIM
    bq = min(bq, s)
    bk = bq // 2
    kv_spec = lambda col: pl.BlockSpec((None, s, d), lambda bi, h, qi: (bi, 0, col + h))
    return pl.pallas_call(
        functools.partial(_sb_kernel, bq=bq, bk=bk, scale=d ** -0.5),
        grid=(b, n_heads, s // bq),
        in_specs=[pl.BlockSpec((None, bq, d), lambda bi, h, qi: (bi, qi, q_col + h)),
                  kv_spec(k_col), kv_spec(v_col)],
        out_specs=pl.BlockSpec((None, bq, d), lambda bi, h, qi: (bi, qi, h)),
        out_shape=jax.ShapeDtypeStruct((b, s, n_heads * d), BF16),
        scratch_shapes=[pltpu.VMEM((s // bk, d, bk), BF16), pltpu.VMEM((1, bq), F32),
                        pltpu.VMEM((d, bq), F32)],
        compiler_params=_params(3), name="sb_attention")(proj, proj, proj)


def _rope_kernel(ang_ref, cos_ref, sin_ref):
    ang = ang_ref[...]
    cos_ref[...] = jnp.cos(ang)
    sin_ref[...] = jnp.sin(ang)


def _rope_table(positions):
    b, s = positions.shape
    half = QK_ROPE_DIM // 2
    per_row = LANES // half
    inv_freq = ROPE_THETA ** (-jnp.arange(half, dtype=F32) / half)
    pos = jnp.repeat(positions.astype(F32), half, axis=-1).reshape(b * s // per_row, LANES)
    ang_in = pos * jnp.tile(inv_freq, per_row)[None, :]
    rows = ang_in.shape[0]
    spec = pl.BlockSpec((rows, LANES), lambda i: (0, 0))
    cos, sin = pl.pallas_call(
        _rope_kernel, grid=(1,), in_specs=[spec], out_specs=[spec, spec],
        out_shape=[jax.ShapeDtypeStruct((rows, LANES), F32)] * 2,
        compiler_params=_params(1), name="rope_table")(ang_in)
    cos = cos.reshape(b, s, half)
    sin = sin.reshape(b, s, half)
    return jnp.concatenate([cos, cos, -sin, sin], axis=-1)


def _mlp(x, g, w_up, w_down, layer):
    a = _rms_matmul(x, g, w_up, layer=layer, relu2=True, tn=2048, name="mlp%d_up" % layer)
    return _mm_res([a], w_down, x, layer=layer, tk=2048, name="mlp%d_down" % layer)


def _sb_fox_layer(x, batch, seq, g, w_in, i, b_f, fox_q_g, fox_k_g, w_o):
    d_model = x.shape[1]
    n_heads = d_model // (2 * HEAD_DIM)
    width = n_heads * HEAD_DIM
    w_f = jnp.pad(w_in[i, :, 6 * width:], ((0, 0), (0, LANES - n_heads))).astype(BF16)
    proj, f_logit = _rms_matmul(x, g, w_in, layer=i, n=6 * width, w_aux=w_f, tn=2048,
                                name="in_proj")
    b_pad = jnp.pad(b_f.astype(F32), (0, LANES - n_heads)).reshape(1, LANES)
    f_aug = _forget_cumsum(f_logit, b_pad, seq, n_heads).reshape(batch, seq, LANES)
    proj = proj.reshape(batch, seq, 6 * width)
    o_sb = _sb_attention(proj, n_heads=n_heads, q_col=0, k_col=n_heads, v_col=2 * n_heads)
    o_fx = _fox_attention(proj, f_aug, fox_q_g, fox_k_g, n_heads=n_heads,
                          q_col=3 * n_heads, k_col=4 * n_heads, v_col=5 * n_heads)
    return _mm_res([o_sb.reshape(-1, width), o_fx.reshape(-1, width)], w_o, x, layer=i,
                   name="sf_out_proj")


def _swap_halves(w):
    half = w.shape[-1] // 2
    return jnp.concatenate([w[..., half:], w[..., :half]], axis=-1)


def _mla_layer(x, batch, seq, tab, g, w_down, i, q_a_g, kv_a_g, w_uq, w_ukv, q_g, k_g, w_o):
    lora = Q_LORA_RANK + KV_LORA_RANK
    w_pe = w_down[:, lora:]
    w_down_ext = jnp.concatenate([w_down, _swap_halves(w_pe)], axis=1).astype(BF16)
    down = _rms_matmul(x, g, w_down_ext, out_dtype=F32, tn=w_down_ext.shape[1], name="mla_down")
    w_uq_h = w_uq.reshape(Q_LORA_RANK, N_MLA_HEADS, QK_HEAD_DIM)
    w_uq_ext = jnp.concatenate([w_uq_h, _swap_halves(w_uq_h[..., QK_NOPE_DIM:])], axis=-1)
    w_uq_ext = w_uq_ext.reshape(Q_LORA_RANK, -1).astype(BF16)
    q_ext = _rms_matmul(down, q_a_g, w_uq_ext, xcol=0, tm=2048, name="mla_uq")
    kv = _rms_matmul(down, kv_a_g, w_ukv, layer=i, xcol=1, tm=2048, name="mla_ukv")
    o = _mla_attention(q_ext.reshape(batch, seq, -1), kv.reshape(batch, seq, -1),
                       down.reshape(batch, seq, -1), tab, q_g, k_g, kp_col=lora // LANES)
    return _mm_res([o.reshape(batch * seq, -1)], w_o, x, layer=i, tk=2048, name="mla_out_proj")


def kernel(x, positions, ln_mix_g, ln_mlp_g, sf_w_in, sf_b_f, fox_q_g, fox_k_g, sf_w_o,
           mla_w_down, mla_q_a_g, mla_kv_a_g, mla_w_uq, mla_w_ukv, mla_q_g, mla_k_g,
           mla_w_o, mlp_w_up, mlp_w_down):
    batch, seq, d_model = x.shape
    depth = ln_mix_g.shape[0]
    tab = _rope_table(positions)
    sf_w_in, sf_w_o, mla_w_ukv, mla_w_o, mlp_w_up, mlp_w_down = (
        w.astype(BF16) for w in (sf_w_in, sf_w_o, mla_w_ukv, mla_w_o, mlp_w_up, mlp_w_down))
    h = x.reshape(batch * seq, d_model)
    for layer in range(depth):
        i = layer // 2
        if layer % 2 == 0:
            h = _sb_fox_layer(h, batch, seq, ln_mix_g[layer], sf_w_in, i, sf_b_f[i],
                              fox_q_g[i], fox_k_g[i], sf_w_o)
        else:
            h = _mla_layer(h, batch, seq, tab, ln_mix_g[layer], mla_w_down[i], i, mla_q_a_g[i],
                           mla_kv_a_g[i], mla_w_uq[i], mla_w_ukv, mla_q_g[i], mla_k_g[i],
                           mla_w_o)
        h = _mlp(h, ln_mlp_g[layer], mlp_w_up, mlp_w_down, layer)
    return h.reshape(batch, seq, d_model)
```

```python
import functools

import jax
import jax.numpy as jnp
import numpy as np
from jax import lax
from jax.experimental import pallas as pl
from jax.experimental.pallas import tpu as pltpu

F32 = jnp.float32
BF16 = jnp.bfloat16

HEAD_DIM = 128
N_MLA_HEADS = 16
Q_LORA_RANK = 512
KV_LORA_RANK = 512
QK_NOPE_DIM = 128
QK_ROPE_DIM = 64
QK_HEAD_DIM = QK_NOPE_DIM + QK_ROPE_DIM
V_HEAD_DIM = 128
ROPE_THETA = 10000.0
EPS = 1e-6
LOG2E = 1.4426950408889634

LANES = 128
VMEM_LIMIT = 56 * 1024 * 1024
ARB = "arbitrary"


def _params(n_axes):
    return pltpu.CompilerParams(dimension_semantics=(ARB,) * n_axes,
                                vmem_limit_bytes=VMEM_LIMIT)


def _rms(x, g):
    ms = jnp.mean(x * x, axis=-1, keepdims=True)
    return x * lax.rsqrt(ms + EPS) * g


def _rms_matmul_kernel(x_ref, g_ref, w_ref, *rest, relu2, aux):
    if aux:
        wa_ref, o_ref, oa_ref, h_scr = rest
    else:
        o_ref, h_scr = rest

    @pl.when(pl.program_id(1) == 0)
    def _():
        h = _rms(x_ref[...].astype(F32), g_ref[...]).astype(BF16)
        h_scr[...] = h
        if aux:
            oa_ref[...] = jnp.dot(h, wa_ref[...], preferred_element_type=F32)

    acc = jnp.dot(h_scr[...], w_ref[...].astype(BF16), preferred_element_type=F32)
    if relu2:
        acc = jnp.square(jnp.maximum(acc, 0.0))
    o_ref[...] = acc.astype(o_ref.dtype)


def _weight_spec(w, layer, block, index):
    if w.ndim == 2:
        return pl.BlockSpec(block, index)
    return pl.BlockSpec((None,) + block, lambda *ids: (layer,) + index(*ids))


def _rms_matmul(x, g, w, *, layer=None, n=None, xcol=0, out_dtype=BF16, relu2=False,
                w_aux=None, tm=1024, tn=1024, name):
    t = x.shape[0]
    k = w.shape[-2]
    n = w.shape[-1] if n is None else n
    tm, tn = min(tm, t), min(tn, n)
    assert t % tm == 0 and n % tn == 0
    aux = w_aux is not None
    in_specs = [pl.BlockSpec((tm, k), lambda i, j: (i, xcol)),
                pl.BlockSpec((1, k), lambda i, j: (0, 0)),
                _weight_spec(w, layer, (k, tn), lambda i, j: (0, j))]
    out_specs = pl.BlockSpec((tm, tn), lambda i, j: (i, j))
    out_shape = jax.ShapeDtypeStruct((t, n), out_dtype)
    args = [x, g.reshape(1, k).astype(F32), w]
    if aux:
        na = w_aux.shape[1]
        in_specs.append(pl.BlockSpec((k, na), lambda i, j: (0, 0)))
        out_specs = [out_specs, pl.BlockSpec((tm, na), lambda i, j: (i, 0))]
        out_shape = [out_shape, jax.ShapeDtypeStruct((t, na), F32)]
        args.append(w_aux)
    return pl.pallas_call(
        functools.partial(_rms_matmul_kernel, relu2=relu2, aux=aux),
        grid=(t // tm, n // tn),
        in_specs=in_specs, out_specs=out_specs, out_shape=out_shape,
        scratch_shapes=[pltpu.VMEM((tm, k), BF16)],
        compiler_params=_params(2), name=name)(*args)


def _mm_res_kernel(*refs, n_pairs):
    a_refs = refs[:n_pairs]
    w_refs = refs[n_pairs:2 * n_pairs]
    r_ref, o_ref = refs[2 * n_pairs:]
    k = pl.program_id(2)

    @pl.when(k == 0)
    def _():
        o_ref[...] = r_ref[...]

    acc = jnp.dot(a_refs[0][...], w_refs[0][...].astype(BF16), preferred_element_type=F32)
    for a_ref, w_ref in zip(a_refs[1:], w_refs[1:]):
        acc += jnp.dot(a_ref[...], w_ref[...].astype(BF16), preferred_element_type=F32)
    o_ref[...] += acc


def _mm_res(a_list, w, r, *, layer=None, tm=1024, tn=1024, tk=1024, name):
    t, n = r.shape
    k = a_list[0].shape[1]
    tm, tn, tk = min(tm, t), min(tn, n), min(tk, k)
    assert t % tm == 0 and n % tn == 0 and k % tk == 0
    n_pairs = len(a_list)
    assert w.shape[-2] == n_pairs * k
    k_blocks = k // tk
    in_specs = ([pl.BlockSpec((tm, tk), lambda i, j, kk: (i, kk))] * n_pairs
                + [_weight_spec(w, layer, (tk, tn),
                                lambda i, j, kk, p=p: (p * k_blocks + kk, j))
                   for p in range(n_pairs)]
                + [pl.BlockSpec((tm, tn), lambda i, j, kk: (i, j))])
    return pl.pallas_call(
        functools.partial(_mm_res_kernel, n_pairs=n_pairs),
        grid=(t // tm, n // tn, k // tk),
        in_specs=in_specs,
        out_specs=pl.BlockSpec((tm, tn), lambda i, j, kk: (i, j)),
        out_shape=jax.ShapeDtypeStruct((t, n), F32),
        compiler_params=_params(3), name=name)(*a_list, *([w] * n_pairs), r)


def _log_sigmoid(z):
    return jnp.minimum(z, 0.0) - jnp.log(1.0 + jnp.exp(-jnp.abs(z)))


def _split3(x):
    x1 = x.astype(BF16)
    r1 = x - x1.astype(F32)
    x2 = r1.astype(BF16)
    x3 = (r1 - x2.astype(F32)).astype(BF16)
    return x1, x2, x3


def _forget_cumsum_kernel(f_ref, b_ref, o_ref, *, n_heads, chunk):
    s = f_ref.shape[0]
    row = lax.broadcasted_iota(jnp.int32, (chunk, chunk), 0)
    col = lax.broadcasted_iota(jnp.int32, (chunk, chunk), 1)
    lower = jnp.where(col <= row, 1.0, 0.0).astype(BF16)
    lane = lax.broadcasted_iota(jnp.int32, (chunk, LANES), 1)

    def body(c, carry):
        rows = pl.ds(pl.multiple_of(c * chunk, chunk), chunk)
        lf = jnp.where(lane < n_heads, _log_sigmoid(f_ref[rows, :] + b_ref[...]), 0.0)
        cs = carry
        for part in _split3(lf):
            cs = cs + jnp.dot(lower, part, preferred_element_type=F32)
        hi, mid, lo = _split3(-LOG2E * cs)
        packed = (hi.astype(F32) + pltpu.roll(mid.astype(F32), n_heads, 1)
                  + pltpu.roll(lo.astype(F32), 2 * n_heads, 1))
        o_ref[rows, :] = packed.astype(BF16)
        return cs[chunk - 1:chunk, :]

    lax.fori_loop(0, s // chunk, body, jnp.zeros((1, LANES), F32))


def _forget_cumsum(f_logit, b_pad, seq, n_heads):
    t = f_logit.shape[0]
    return pl.pallas_call(
        functools.partial(_forget_cumsum_kernel, n_heads=n_heads, chunk=min(512, seq)),
        grid=(t // seq,),
        in_specs=[pl.BlockSpec((seq, LANES), lambda b: (b, 0)),
                  pl.BlockSpec((1, LANES), lambda b: (0, 0))],
        out_specs=pl.BlockSpec((seq, LANES), lambda b: (b, 0)),
        out_shape=jax.ShapeDtypeStruct((t, LANES), BF16),
        compiler_params=_params(1), name="forget_cumsum")(f_logit, b_pad)


def _softmax_attention(qb, k_scr, vt_scr, o_ref, s_a, s_b, m_scr, l_scr, acc_scr, *, bq, bk):
    nk = bq // bk
    assert bq == nk * bk and nk % 2 == 0
    qi = pl.program_id(2)
    m_scr[...] = jnp.full(m_scr.shape, -jnp.inf, F32)
    l_scr[...] = jnp.zeros(l_scr.shape, F32)
    acc_scr[...] = jnp.zeros(acc_scr.shape, F32)
    n_full = nk * qi
    bufs = (s_a, s_b)

    def scores(c, q0):
        rows = pl.ds(pl.multiple_of(c * bk, bk), bk)
        return lax.dot_general(k_scr[rows, :], qb[q0:, :], (((1,), (1,)), ((), ())),
                               preferred_element_type=F32)

    def consume(s_ref, c, q0, masked):
        s = s_ref[:, q0:]
        if masked:
            key = lax.broadcasted_iota(jnp.int32, s.shape, 0)
            query = lax.broadcasted_iota(jnp.int32, s.shape, 1)
            s = jnp.where(key <= query, s, -jnp.inf)
        m_prev = m_scr[:, q0:]
        m_new = jnp.maximum(m_prev, jnp.max(s, axis=0, keepdims=True))
        alpha = jnp.exp2(m_prev - m_new)
        p = jnp.exp2(s - m_new)
        l_scr[:, q0:] = alpha * l_scr[:, q0:] + jnp.sum(p, axis=0, keepdims=True)
        acc_scr[:, q0:] = alpha * acc_scr[:, q0:] + jnp.dot(
            vt_scr[c], p.astype(BF16), preferred_element_type=F32)
        m_scr[:, q0:] = m_new

    s_a[...] = scores(0, 0)

    def group(i, carry):
        c = nk * i
        for j in range(nk):
            bufs[(j + 1) % 2][...] = scores(c + j + 1, 0)
            consume(bufs[j % 2], c + j, 0, False)
        return carry

    lax.fori_loop(0, qi, group, 0)
    for j in range(nk):
        if j + 1 < nk:
            bufs[(j + 1) % 2][:, (j + 1) * bk:] = scores(n_full + j + 1, (j + 1) * bk)
        consume(bufs[j % 2], n_full + j, j * bk, True)
    o_ref[...] = (acc_scr[...] * (1.0 / l_scr[...])).T.astype(o_ref.dtype)


def _attn_scratch(seq, bq, bk, dk, dv):
    return [pltpu.VMEM((seq, dk), BF16), pltpu.VMEM((seq // bk, dv, bk), BF16),
            pltpu.VMEM((bk, bq), F32), pltpu.VMEM((bk, bq), F32),
            pltpu.VMEM((1, bq), F32), pltpu.VMEM((1, bq), F32), pltpu.VMEM((dv, bq), F32)]


def _lane_sumsq(x, n_valid):
    row = lax.broadcasted_iota(jnp.int32, (x.shape[-1], LANES), 0)
    sel = jnp.where(row < n_valid, 1.0, 0.0).astype(BF16)
    return jnp.dot((x * x).astype(BF16), sel, preferred_element_type=F32)


def _rms_mxu(x, g):
    return x * lax.rsqrt(_lane_sumsq(x, x.shape[-1]) / x.shape[-1] + EPS) * g


def _prep_values(v_ref, vt_scr, c, rows):
    vt_scr[c] = v_ref[rows, :].astype(F32).T.astype(BF16)


def _fox_kernel(q_ref, k_ref, v_ref, fa_ref, gq_ref, gk_ref, o_ref,
                k_scr, vt_scr, s_a, s_b, m_scr, l_scr, acc_scr, *, bq, bk, n_heads):
    d = HEAD_DIM
    h = pl.program_id(1)

    @pl.when(pl.program_id(2) == 0)
    def _():
        def prep(c, carry):
            rows = pl.ds(pl.multiple_of(c * bk, bk), bk)
            k_scr[rows, :d] = _rms_mxu(k_ref[rows, :].astype(F32), gk_ref[...]).astype(BF16)
            k_scr[rows, d:] = fa_ref[rows, :]
            _prep_values(v_ref, vt_scr, c, rows)
            return carry
        lax.fori_loop(0, k_ref.shape[0] // bk, prep, 0)

    qn = _rms_mxu(q_ref[...].astype(F32), gq_ref[...])
    lane = lax.broadcasted_iota(jnp.int32, (bq, LANES), 1)
    pick = (lane == h) | (lane == h + n_heads) | (lane == h + 2 * n_heads)
    qb = jnp.concatenate([qn.astype(BF16), jnp.where(pick, 1.0, 0.0).astype(BF16)], axis=-1)
    _softmax_attention(qb, k_scr, vt_scr, o_ref, s_a, s_b, m_scr, l_scr, acc_scr, bq=bq, bk=bk)


def _fox_attention(proj, f_aug, gq, gk, *, n_heads, q_col, k_col, v_col, bq=1024, bk=512):
    b, s, _ = proj.shape
    d = HEAD_DIM
    bq = min(bq, s)
    kv_spec = lambda col: pl.BlockSpec((None, s, d), lambda bi, h, qi: (bi, 0, col + h))
    return pl.pallas_call(
        functools.partial(_fox_kernel, bq=bq, bk=bk, n_heads=n_heads),
        grid=(b, n_heads, s // bq),
        in_specs=[pl.BlockSpec((None, bq, d), lambda bi, h, qi: (bi, qi, q_col + h)),
                  kv_spec(k_col), kv_spec(v_col),
                  pl.BlockSpec((None, s, LANES), lambda bi, h, qi: (bi, 0, 0)),
                  pl.BlockSpec((1, d), lambda bi, h, qi: (0, 0)),
                  pl.BlockSpec((1, d), lambda bi, h, qi: (0, 0))],
        out_specs=pl.BlockSpec((None, bq, d), lambda bi, h, qi: (bi, qi, h)),
        out_shape=jax.ShapeDtypeStruct((b, s, n_heads * d), BF16),
        scratch_shapes=_attn_scratch(s, bq, bk, 2 * LANES, d),
        compiler_params=_params(3), name="fox_attention")(
            proj, proj, proj, f_aug, (gq * (d ** -0.5 * LOG2E)).reshape(1, d).astype(F32),
            gk.reshape(1, d).astype(F32))


def _mla_norm_rope(x, tab, g_nope, g_pk, zero_pad):
    nope, pk = x[:, :QK_NOPE_DIM], x[:, QK_NOPE_DIM:]
    r = lax.rsqrt(_lane_sumsq(x, QK_HEAD_DIM) / QK_HEAD_DIM + EPS)
    a = pk * g_pk * tab
    rot = a + pltpu.roll(a, QK_ROPE_DIM, 1)
    if zero_pad:
        lane = lax.broadcasted_iota(jnp.int32, pk.shape, 1)
        rot = jnp.where(lane < QK_ROPE_DIM, rot, 0.0)
    return nope * r * g_nope, rot * r


def _mla_kernel(q_ref, kn_ref, v_ref, kp_ref, tabk_ref, tabq_ref,
                gqn_ref, gqp_ref, gkn_ref, gkp_ref, o_ref,
                k_scr, vt_scr, s_a, s_b, m_scr, l_scr, acc_scr, *, bq, bk):
    d = QK_NOPE_DIM

    @pl.when(pl.program_id(2) == 0)
    def _():
        def prep(c, carry):
            rows = pl.ds(pl.multiple_of(c * bk, bk), bk)
            kx = jnp.concatenate([kn_ref[rows, :].astype(F32), kp_ref[rows, :].astype(F32)],
                                 axis=-1)
            kn, kr = _mla_norm_rope(kx, tabk_ref[rows, :], gkn_ref[...], gkp_ref[...], True)
            k_scr[rows, :d] = kn.astype(BF16)
            k_scr[rows, d:] = kr.astype(BF16)
            _prep_values(v_ref, vt_scr, c, rows)
            return carry
        lax.fori_loop(0, kn_ref.shape[0] // bk, prep, 0)

    qn, qr = _mla_norm_rope(q_ref[...].astype(F32), tabq_ref[...], gqn_ref[...], gqp_ref[...],
                            False)
    qb = jnp.concatenate([qn.astype(BF16), qr.astype(BF16)], axis=-1)
    _softmax_attention(qb, k_scr, vt_scr, o_ref, s_a, s_b, m_scr, l_scr, acc_scr, bq=bq, bk=bk)


def _mla_attention(q_ext, kv, down, tab, gq, gk, *, kp_col, bq=1024, bk=512):
    b, s, _ = q_ext.shape
    h_n = N_MLA_HEADS
    d = QK_NOPE_DIM
    bq = min(bq, s)

    def pack_gain(g):
        g_pe = g[d:]
        half = QK_ROPE_DIM // 2
        g_sw = jnp.concatenate([g_pe[half:], g_pe[:half]])
        return (g[:d].reshape(1, d).astype(F32),
                jnp.concatenate([g_pe, g_sw]).reshape(1, LANES).astype(F32))

    gqn, gqp = pack_gain(gq * (QK_HEAD_DIM ** -0.5 * LOG2E))
    gkn, gkp = pack_gain(gk)
    full = lambda col_fn: pl.BlockSpec((None, s, LANES), lambda bi, h, qi: (bi, 0, col_fn(h)))
    vec = pl.BlockSpec((1, LANES), lambda bi, h, qi: (0, 0))
    return pl.pallas_call(
        functools.partial(_mla_kernel, bq=bq, bk=bk),
        grid=(b, h_n, s // bq),
        in_specs=[pl.BlockSpec((None, bq, 2 * LANES), lambda bi, h, qi: (bi, qi, h)),
                  full(lambda h: 2 * h), full(lambda h: 2 * h + 1),
                  full(lambda h: kp_col), full(lambda h: 0),
                  pl.BlockSpec((None, bq, LANES), lambda bi, h, qi: (bi, qi, 0)),
                  vec, vec, vec, vec],
        out_specs=pl.BlockSpec((None, bq, V_HEAD_DIM), lambda bi, h, qi: (bi, qi, h)),
        out_shape=jax.ShapeDtypeStruct((b, s, h_n * V_HEAD_DIM), BF16),
        scratch_shapes=_attn_scratch(s, bq, bk, 2 * LANES, V_HEAD_DIM),
        compiler_params=_params(3), name="mla_attention")(
            q_ext, kv, kv, down, tab, tab, gqn, gqp, gkn, gkp)


F32_EXP2_ZERO = -150.0


def _sb_kernel(q_ref, k_ref, v_ref, o_ref, vt_scr, r_scr, acc_scr, *, bq, bk, scale):
    assert bq == 2 * bk
    qi = pl.program_id(2)

    @pl.when(qi == 0)
    def _():
        def prep(c, carry):
            _prep_values(v_ref, vt_scr, c, pl.ds(pl.multiple_of(c * bk, bk), bk))
            return carry
        lax.fori_loop(0, k_ref.shape[0] // bk, prep, 0)

    qb = (q_ref[...].astype(F32) * (scale * LOG2E)).astype(BF16)
    row = lax.broadcasted_iota(jnp.int32, (bk, bk), 0)
    col = lax.broadcasted_iota(jnp.int32, (bk, bk), 1)
    later = jnp.where(col > row, 1.0, 0.0).astype(BF16)

    def scan_chunk(c, q0, masked):
        rows = pl.ds(pl.multiple_of(c * bk, bk), bk)
        z = lax.dot_general(k_ref[rows, :], qb[q0:, :], (((1,), (1,)), ((), ())),
                            preferred_element_type=F32)
        log_beta = jnp.minimum(z, 0.0) - jnp.log2(1.0 + jnp.exp2(-jnp.abs(z)))
        log_rest = log_beta - z
        valid = None
        if masked:
            key = lax.broadcasted_iota(jnp.int32, z.shape, 0)
            query = lax.broadcasted_iota(jnp.int32, z.shape, 1)
            valid = key < query
            log_rest = jnp.where(valid, log_rest, 0.0)
        hi = log_rest.astype(BF16)
        lo = (log_rest - hi.astype(F32)).astype(BF16)
        suffix = (jnp.dot(later, hi, preferred_element_type=F32)
                  + jnp.dot(later, lo, preferred_element_type=F32))
        return log_beta + suffix, suffix[0:1, :] + log_rest[0:1, :], valid

    def weights(e, r, valid):
        w = jnp.exp2(e + r)
        if valid is not None:
            w = jnp.where(valid, w, 0.0)
        return w.astype(BF16)

    def window(with_low):
        e_top, t_top, v_top = scan_chunk(2 * qi + 1, bk, True)
        e_mid, t_mid, v_mid = scan_chunk(2 * qi, 0, True)
        if with_low:
            e_low, t_low, _ = scan_chunk(2 * qi - 1, 0, False)
        w_top = weights(e_top, 0.0, v_top)
        r = jnp.concatenate([jnp.zeros((1, bk), F32), t_top], axis=1)
        w_mid = weights(e_mid, r, v_mid)
        r = r + t_mid
        acc = jnp.dot(vt_scr[2 * qi], w_mid, preferred_element_type=F32)
        if with_low:
            w_low = weights(e_low, r, None)
            r = r + t_low
            acc = acc + jnp.dot(vt_scr[2 * qi - 1], w_low, preferred_element_type=F32)
        acc_top = jnp.dot(vt_scr[2 * qi + 1], w_top, preferred_element_type=F32)
        acc_scr[:, :bk] = acc[:, :bk]
        acc_scr[:, bk:] = acc[:, bk:] + acc_top
        r_scr[...] = r

    @pl.when(qi == 0)
    def _():
        window(False)

    @pl.when(qi > 0)
    def _():
        window(True)

    def more(c):
        return jnp.logical_and(c >= 0, jnp.max(r_scr[...]) > F32_EXP2_ZERO)

    def body(c):
        e, t, _ = scan_chunk(c, 0, False)
        r_prev = r_scr[...]
        acc_scr[...] += jnp.dot(vt_scr[c], weights(e, r_prev, None),
                                preferred_element_type=F32)
        r_scr[...] = r_prev + t
        return c - 1

    lax.while_loop(more, body, 2 * qi - 2)
    o_ref[...] = acc_scr[...].T.astype(o_ref.dtype)


def _sb_attention(proj, *, n_heads, q_col, k_col, v_col, bq=512):
    b, s, _ = proj.shape
    d = HEAD_DIM
    bq = min(bq, s)
    bk = bq // 2
    kv_spec = lambda col: pl.BlockSpec((None, s, d), lambda bi, h, qi: (bi, 0, col + h))
    return pl.pallas_call(
        functools.partial(_sb_kernel, bq=bq, bk=bk, scale=d ** -0.5),
        grid=(b, n_heads, s // bq),
        in_specs=[pl.BlockSpec((None, bq, d), lambda bi, h, qi: (bi, qi, q_col + h)),
                  kv_spec(k_col), kv_spec(v_col)],
        out_specs=pl.BlockSpec((None, bq, d), lambda bi, h, qi: (bi, qi, h)),
        out_shape=jax.ShapeDtypeStruct((b, s, n_heads * d), BF16),
        scratch_shapes=[pltpu.VMEM((s // bk, d, bk), BF16), pltpu.VMEM((1, bq), F32),
                        pltpu.VMEM((d, bq), F32)],
        compiler_params=_params(3), name="sb_attention")(proj, proj, proj)


def _rope_kernel(ang_ref, cos_ref, sin_ref):
    ang = ang_ref[...]
    cos_ref[...] = jnp.cos(ang)
    sin_ref[...] = jnp.sin(ang)


def _rope_table(positions):
    b, s = positions.shape
    half = QK_ROPE_DIM // 2
    per_row = LANES // half
    inv_freq = ROPE_THETA ** (-jnp.arange(half, dtype=F32) / half)
    pos = jnp.repeat(positions.astype(F32), half, axis=-1).reshape(b * s // per_row, LANES)
    ang_in = pos * jnp.tile(inv_freq, per_row)[None, :]
    rows = ang_in.shape[0]
    spec = pl.BlockSpec((rows, LANES), lambda i: (0, 0))
    cos, sin = pl.pallas_call(
        _rope_kernel, grid=(1,), in_specs=[spec], out_specs=[spec, spec],
        out_shape=[jax.ShapeDtypeStruct((rows, LANES), F32)] * 2,
        compiler_params=_params(1), name="rope_table")(ang_in)
    cos = cos.reshape(b, s, half)
    sin = sin.reshape(b, s, half)
    return jnp.concatenate([cos, cos, -sin, sin], axis=-1)


def _mlp(x, g, w_up, w_down, layer):
    a = _rms_matmul(x, g, w_up, layer=layer, relu2=True, tn=2048, name="mlp%d_up" % layer)
    return _mm_res([a], w_down, x, layer=layer, tk=2048, name="mlp%d_down" % layer)


def _sb_fox_layer(x, batch, seq, g, w_in, i, b_f, fox_q_g, fox_k_g, w_o):
    d_model = x.shape[1]
    n_heads = d_model // (2 * HEAD_DIM)
    width = n_heads * HEAD_DIM
    w_f = jnp.pad(w_in[i, :, 6 * width:], ((0, 0), (0, LANES - n_heads))).astype(BF16)
    proj, f_logit = _rms_matmul(x, g, w_in, layer=i, n=6 * width, w_aux=w_f, tn=2048,
                                name="in_proj")
    b_pad = jnp.pad(b_f.astype(F32), (0, LANES - n_heads)).reshape(1, LANES)
    f_aug = _forget_cumsum(f_logit, b_pad, seq, n_heads).reshape(batch, seq, LANES)
    proj = proj.reshape(batch, seq, 6 * width)
    o_sb = _sb_attention(proj, n_heads=n_heads, q_col=0, k_col=n_heads, v_col=2 * n_heads)
    o_fx = _fox_attention(proj, f_aug, fox_q_g, fox_k_g, n_heads=n_heads,
                          q_col=3 * n_heads, k_col=4 * n_heads, v_col=5 * n_heads)
    return _mm_res([o_sb.reshape(-1, width), o_fx.reshape(-1, width)], w_o, x, layer=i,
                   name="sf_out_proj")


def _swap_halves(w):
    half = w.shape[-1] // 2
    return jnp.concatenate([w[..., half:], w[..., :half]], axis=-1)


def _mla_layer(x, batch, seq, tab, g, w_down, i, q_a_g, kv_a_g, w_uq, w_ukv, q_g, k_g, w_o):
    lora = Q_LORA_RANK + KV_LORA_RANK
    w_pe = w_down[:, lora:]
    w_down_ext = jnp.concatenate([w_down, _swap_halves(w_pe)], axis=1).astype(BF16)
    down = _rms_matmul(x, g, w_down_ext, out_dtype=F32, tn=w_down_ext.shape[1], name="mla_down")
    w_uq_h = w_uq.reshape(Q_LORA_RANK, N_MLA_HEADS, QK_HEAD_DIM)
    w_uq_ext = jnp.concatenate([w_uq_h, _swap_halves(w_uq_h[..., QK_NOPE_DIM:])], axis=-1)
    w_uq_ext = w_uq_ext.reshape(Q_LORA_RANK, -1).astype(BF16)
    q_ext = _rms_matmul(down, q_a_g, w_uq_ext, xcol=0, tm=2048, name="mla_uq")
    kv = _rms_matmul(down, kv_a_g, w_ukv, layer=i, xcol=1, tm=2048, name="mla_ukv")
    o = _mla_attention(q_ext.reshape(batch, seq, -1), kv.reshape(batch, seq, -1),
                       down.reshape(batch, seq, -1), tab, q_g, k_g, kp_col=lora // LANES)
    return _mm_res([o.reshape(batch * seq, -1)], w_o, x, layer=i, tk=2048, name="mla_out_proj")


def kernel(x, positions, ln_mix_g, ln_mlp_g, sf_w_in, sf_b_f, fox_q_g, fox_k_g, sf_w_o,
           mla_w_down, mla_q_a_g, mla_kv_a_g, mla_w_uq, mla_w_ukv, mla_q_g, mla_k_g,
           mla_w_o, mlp_w_up, mlp_w_down):
    batch, seq, d_model = x.shape
    depth = ln_mix_g.shape[0]
    tab = _rope_table(positions)
    sf_w_in, sf_w_o, mla_w_ukv, mla_w_o, mlp_w_up, mlp_w_down = (
        w.astype(BF16) for w in (sf_w_in, sf_w_o, mla_w_ukv, mla_w_o, mlp_w_up, mlp_w_down))
    h = x.reshape(batch * seq, d_model)
    for layer in range(depth):
        i = layer // 2
        if layer % 2 == 0:
            h = _sb_fox_layer(h, batch, seq, ln_mix_g[layer], sf_w_in, i, sf_b_f[i],
                              fox_q_g[i], fox_k_g[i], sf_w_o)
        else:
            h = _mla_layer(h, batch, seq, tab, ln_mix_g[layer], mla_w_down[i], i, mla_q_a_g[i],
                           mla_kv_a_g[i], mla_w_uq[i], mla_w_ukv, mla_q_g[i], mla_k_g[i],
                           mla_w_o)
        h = _mlp(h, ln_mlp_g[layer], mlp_w_up, mlp_w_down, layer)
    return h.reshape(batch, seq, d_model)
```

```python
import functools

import jax
import jax.numpy as jnp
import numpy as np
from jax import lax
from jax.experimental import pallas as pl
from jax.experimental.pallas import tpu as pltpu

F32 = jnp.float32
BF16 = jnp.bfloat16

HEAD_DIM = 128
N_MLA_HEADS = 16
Q_LORA_RANK = 512
KV_LORA_RANK = 512
QK_NOPE_DIM = 128
QK_ROPE_DIM = 64
QK_HEAD_DIM = QK_NOPE_DIM + QK_ROPE_DIM
V_HEAD_DIM = 128
ROPE_THETA = 10000.0
EPS = 1e-6
LOG2E = 1.4426950408889634

LANES = 128
VMEM_LIMIT = 56 * 1024 * 1024
ARB = "arbitrary"


def _params(n_axes):
    return pltpu.CompilerParams(dimension_semantics=(ARB,) * n_axes,
                                vmem_limit_bytes=VMEM_LIMIT)


def _rms(x, g):
    ms = jnp.mean(x * x, axis=-1, keepdims=True)
    return x * lax.rsqrt(ms + EPS) * g


def _rms_matmul_kernel(x_ref, g_ref, w_ref, *rest, relu2, aux):
    if aux:
        wa_ref, o_ref, oa_ref, h_scr = rest
    else:
        o_ref, h_scr = rest

    @pl.when(pl.program_id(1) == 0)
    def _():
        h = _rms(x_ref[...].astype(F32), g_ref[...]).astype(BF16)
        h_scr[...] = h
        if aux:
            oa_ref[...] = jnp.dot(h, wa_ref[...], preferred_element_type=F32)

    acc = jnp.dot(h_scr[...], w_ref[...].astype(BF16), preferred_element_type=F32)
    if relu2:
        acc = jnp.square(jnp.maximum(acc, 0.0))
    o_ref[...] = acc.astype(o_ref.dtype)


def _weight_spec(w, layer, block, index):
    if w.ndim == 2:
        return pl.BlockSpec(block, index)
    return pl.BlockSpec((None,) + block, lambda *ids: (layer,) + index(*ids))


def _rms_matmul(x, g, w, *, layer=None, n=None, xcol=0, out_dtype=BF16, relu2=False,
                w_aux=None, tm=1024, tn=1024, name):
    t = x.shape[0]
    k = w.shape[-2]
    n = w.shape[-1] if n is None else n
    tm, tn = min(tm, t), min(tn, n)
    assert t % tm == 0 and n % tn == 0
    aux = w_aux is not None
    in_specs = [pl.BlockSpec((tm, k), lambda i, j: (i, xcol)),
                pl.BlockSpec((1, k), lambda i, j: (0, 0)),
                _weight_spec(w, layer, (k, tn), lambda i, j: (0, j))]
    out_specs = pl.BlockSpec((tm, tn), lambda i, j: (i, j))
    out_shape = jax.ShapeDtypeStruct((t, n), out_dtype)
    args = [x, g.reshape(1, k).astype(F32), w]
    if aux:
        na = w_aux.shape[1]
        in_specs.append(pl.BlockSpec((k, na), lambda i, j: (0, 0)))
        out_specs = [out_specs, pl.BlockSpec((tm, na), lambda i, j: (i, 0))]
        out_shape = [out_shape, jax.ShapeDtypeStruct((t, na), F32)]
        args.append(w_aux)
    return pl.pallas_call(
        functools.partial(_rms_matmul_kernel, relu2=relu2, aux=aux),
        grid=(t // tm, n // tn),
        in_specs=in_specs, out_specs=out_specs, out_shape=out_shape,
        scratch_shapes=[pltpu.VMEM((tm, k), BF16)],
        compiler_params=_params(2), name=name)(*args)


def _mm_res_kernel(*refs, n_pairs):
    a_refs = refs[:n_pairs]
    w_refs = refs[n_pairs:2 * n_pairs]
    r_ref, o_ref = refs[2 * n_pairs:]
    k = pl.program_id(2)

    @pl.when(k == 0)
    def _():
        o_ref[...] = r_ref[...]

    acc = jnp.dot(a_refs[0][...], w_refs[0][...].astype(BF16), preferred_element_type=F32)
    for a_ref, w_ref in zip(a_refs[1:], w_refs[1:]):
        acc += jnp.dot(a_ref[...], w_ref[...].astype(BF16), preferred_element_type=F32)
    o_ref[...] += acc


def _mm_res(a_list, w, r, *, layer=None, tm=1024, tn=1024, tk=1024, name):
    t, n = r.shape
    k = a_list[0].shape[1]
    tm, tn, tk = min(tm, t), min(tn, n), min(tk, k)
    assert t % tm == 0 and n % tn == 0 and k % tk == 0
    n_pairs = len(a_list)
    assert w.shape[-2] == n_pairs * k
    k_blocks = k // tk
    in_specs = ([pl.BlockSpec((tm, tk), lambda i, j, kk: (i, kk))] * n_pairs
                + [_weight_spec(w, layer, (tk, tn),
                                lambda i, j, kk, p=p: (p * k_blocks + kk, j))
                   for p in range(n_pairs)]
                + [pl.BlockSpec((tm, tn), lambda i, j, kk: (i, j))])
    return pl.pallas_call(
        functools.partial(_mm_res_kernel, n_pairs=n_pairs),
        grid=(t // tm, n // tn, k // tk),
        in_specs=in_specs,
        out_specs=pl.BlockSpec((tm, tn), lambda i, j, kk: (i, j)),
        out_shape=jax.ShapeDtypeStruct((t, n), F32),
        compiler_params=_params(3), name=name)(*a_list, *([w] * n_pairs), r)


def _log_sigmoid(z):
    return jnp.minimum(z, 0.0) - jnp.log(1.0 + jnp.exp(-jnp.abs(z)))


def _split3(x):
    x1 = x.astype(BF16)
    r1 = x - x1.astype(F32)
    x2 = r1.astype(BF16)
    x3 = (r1 - x2.astype(F32)).astype(BF16)
    return x1, x2, x3


def _forget_cumsum_kernel(f_ref, b_ref, o_ref, *, n_heads, chunk):
    s = f_ref.shape[0]
    row = lax.broadcasted_iota(jnp.int32, (chunk, chunk), 0)
    col = lax.broadcasted_iota(jnp.int32, (chunk, chunk), 1)
    lower = jnp.where(col <= row, 1.0, 0.0).astype(BF16)
    lane = lax.broadcasted_iota(jnp.int32, (chunk, LANES), 1)

    def body(c, carry):
        rows = pl.ds(pl.multiple_of(c * chunk, chunk), chunk)
        lf = jnp.where(lane < n_heads, _log_sigmoid(f_ref[rows, :] + b_ref[...]), 0.0)
        cs = carry
        for part in _split3(lf):
            cs = cs + jnp.dot(lower, part, preferred_element_type=F32)
        hi, mid, lo = _split3(-LOG2E * cs)
        packed = (hi.astype(F32) + pltpu.roll(mid.astype(F32), n_heads, 1)
                  + pltpu.roll(lo.astype(F32), 2 * n_heads, 1))
        o_ref[rows, :] = packed.astype(BF16)
        return cs[chunk - 1:chunk, :]

    lax.fori_loop(0, s // chunk, body, jnp.zeros((1, LANES), F32))


def _forget_cumsum(f_logit, b_pad, seq, n_heads):
    t = f_logit.shape[0]
    return pl.pallas_call(
        functools.partial(_forget_cumsum_kernel, n_heads=n_heads, chunk=min(512, seq)),
        grid=(t // seq,),
        in_specs=[pl.BlockSpec((seq, LANES), lambda b: (b, 0)),
                  pl.BlockSpec((1, LANES), lambda b: (0, 0))],
        out_specs=pl.BlockSpec((seq, LANES), lambda b: (b, 0)),
        out_shape=jax.ShapeDtypeStruct((t, LANES), BF16),
        compiler_params=_params(1), name="forget_cumsum")(f_logit, b_pad)


def _softmax_attention(qb, k_scr, vt_scr, o_ref, s_a, s_b, m_scr, l_scr, acc_scr, *, bq, bk):
    nk = bq // bk
    assert bq == nk * bk and nk % 2 == 0
    qi = pl.program_id(2)
    m_scr[...] = jnp.full(m_scr.shape, -jnp.inf, F32)
    l_scr[...] = jnp.zeros(l_scr.shape, F32)
    acc_scr[...] = jnp.zeros(acc_scr.shape, F32)
    n_full = nk * qi
    bufs = (s_a, s_b)

    def scores(c, q0):
        rows = pl.ds(pl.multiple_of(c * bk, bk), bk)
        return lax.dot_general(k_scr[rows, :], qb[q0:, :], (((1,), (1,)), ((), ())),
                               preferred_element_type=F32)

    def consume(s_ref, c, q0, masked):
        s = s_ref[:, q0:]
        if masked:
            key = lax.broadcasted_iota(jnp.int32, s.shape, 0)
            query = lax.broadcasted_iota(jnp.int32, s.shape, 1)
            s = jnp.where(key <= query, s, -jnp.inf)
        m_prev = m_scr[:, q0:]
        m_new = jnp.maximum(m_prev, jnp.max(s, axis=0, keepdims=True))
        alpha = jnp.exp2(m_prev - m_new)
        p = jnp.exp2(s - m_new)
        l_scr[:, q0:] = alpha * l_scr[:, q0:] + jnp.sum(p, axis=0, keepdims=True)
        acc_scr[:, q0:] = alpha * acc_scr[:, q0:] + jnp.dot(
            vt_scr[c], p.astype(BF16), preferred_element_type=F32)
        m_scr[:, q0:] = m_new

    s_a[...] = scores(0, 0)

    def group(i, carry):
        c = nk * i
        for j in range(nk):
            bufs[(j + 1) % 2][...] = scores(c + j + 1, 0)
            consume(bufs[j % 2], c + j, 0, False)
        return carry

    lax.fori_loop(0, qi, group, 0)
    for j in range(nk):
        if j + 1 < nk:
            bufs[(j + 1) % 2][:, (j + 1) * bk:] = scores(n_full + j + 1, (j + 1) * bk)
        consume(bufs[j % 2], n_full + j, j * bk, True)
    o_ref[...] = (acc_scr[...] * (1.0 / l_scr[...])).T.astype(o_ref.dtype)


def _attn_scratch(seq, bq, bk, dk, dv):
    return [pltpu.VMEM((seq, dk), BF16), pltpu.VMEM((seq // bk, dv, bk), BF16),
            pltpu.VMEM((bk, bq), F32), pltpu.VMEM((bk, bq), F32),
            pltpu.VMEM((1, bq), F32), pltpu.VMEM((1, bq), F32), pltpu.VMEM((dv, bq), F32)]


def _lane_sumsq(x, n_valid):
    row = lax.broadcasted_iota(jnp.int32, (x.shape[-1], LANES), 0)
    sel = jnp.where(row < n_valid, 1.0, 0.0).astype(BF16)
    return jnp.dot((x * x).astype(BF16), sel, preferred_element_type=F32)


def _rms_mxu(x, g):
    return x * lax.rsqrt(_lane_sumsq(x, x.shape[-1]) / x.shape[-1] + EPS) * g


def _prep_values(v_ref, vt_scr, c, rows):
    vt_scr[c] = v_ref[rows, :].T


def _fox_kernel(q_ref, k_ref, v_ref, fa_ref, gq_ref, gk_ref, o_ref,
                k_scr, vt_scr, s_a, s_b, m_scr, l_scr, acc_scr, *, bq, bk, n_heads):
    d = HEAD_DIM
    h = pl.program_id(1)

    @pl.when(pl.program_id(2) == 0)
    def _():
        def prep(c, carry):
            rows = pl.ds(pl.multiple_of(c * bk, bk), bk)
            k_scr[rows, :d] = _rms_mxu(k_ref[rows, :].astype(F32), gk_ref[...]).astype(BF16)
            k_scr[rows, d:] = fa_ref[rows, :]
            _prep_values(v_ref, vt_scr, c, rows)
            return carry
        lax.fori_loop(0, k_ref.shape[0] // bk, prep, 0)

    qn = _rms_mxu(q_ref[...].astype(F32), gq_ref[...])
    lane = lax.broadcasted_iota(jnp.int32, (bq, LANES), 1)
    pick = (lane == h) | (lane == h + n_heads) | (lane == h + 2 * n_heads)
    qb = jnp.concatenate([qn.astype(BF16), jnp.where(pick, 1.0, 0.0).astype(BF16)], axis=-1)
    _softmax_attention(qb, k_scr, vt_scr, o_ref, s_a, s_b, m_scr, l_scr, acc_scr, bq=bq, bk=bk)


def _fox_attention(proj, f_aug, gq, gk, *, n_heads, q_col, k_col, v_col, bq=2048, bk=512):
    b, s, _ = proj.shape
    d = HEAD_DIM
    bq = min(bq, s)
    kv_spec = lambda col: pl.BlockSpec((None, s, d), lambda bi, h, qi: (bi, 0, col + h))
    return pl.pallas_call(
        functools.partial(_fox_kernel, bq=bq, bk=bk, n_heads=n_heads),
        grid=(b, n_heads, s // bq),
        in_specs=[pl.BlockSpec((None, bq, d), lambda bi, h, qi: (bi, qi, q_col + h)),
                  kv_spec(k_col), kv_spec(v_col),
                  pl.BlockSpec((None, s, LANES), lambda bi, h, qi: (bi, 0, 0)),
                  pl.BlockSpec((1, d), lambda bi, h, qi: (0, 0)),
                  pl.BlockSpec((1, d), lambda bi, h, qi: (0, 0))],
        out_specs=pl.BlockSpec((None, bq, d), lambda bi, h, qi: (bi, qi, h)),
        out_shape=jax.ShapeDtypeStruct((b, s, n_heads * d), BF16),
        scratch_shapes=_attn_scratch(s, bq, bk, 2 * LANES, d),
        compiler_params=_params(3), name="fox_attention")(
            proj, proj, proj, f_aug, (gq * (d ** -0.5 * LOG2E)).reshape(1, d).astype(F32),
            gk.reshape(1, d).astype(F32))


def _mla_norm_rope(x, tab, g_nope, g_pk, zero_pad):
    nope, pk = x[:, :QK_NOPE_DIM], x[:, QK_NOPE_DIM:]
    r = lax.rsqrt(_lane_sumsq(x, QK_HEAD_DIM) / QK_HEAD_DIM + EPS)
    a = pk * g_pk * tab
    rot = a + pltpu.roll(a, QK_ROPE_DIM, 1)
    if zero_pad:
        lane = lax.broadcasted_iota(jnp.int32, pk.shape, 1)
        rot = jnp.where(lane < QK_ROPE_DIM, rot, 0.0)
    return nope * r * g_nope, rot * r


def _mla_kernel(q_ref, kn_ref, v_ref, kp_ref, tabk_ref, tabq_ref,
                gqn_ref, gqp_ref, gkn_ref, gkp_ref, o_ref,
                k_scr, vt_scr, s_a, s_b, m_scr, l_scr, acc_scr, *, bq, bk):
    d = QK_NOPE_DIM

    @pl.when(pl.program_id(2) == 0)
    def _():
        def prep(c, carry):
            rows = pl.ds(pl.multiple_of(c * bk, bk), bk)
            kx = jnp.concatenate([kn_ref[rows, :].astype(F32), kp_ref[rows, :].astype(F32)],
                                 axis=-1)
            kn, kr = _mla_norm_rope(kx, tabk_ref[rows, :], gkn_ref[...], gkp_ref[...], True)
            k_scr[rows, :d] = kn.astype(BF16)
            k_scr[rows, d:] = kr.astype(BF16)
            _prep_values(v_ref, vt_scr, c, rows)
            return carry
        lax.fori_loop(0, kn_ref.shape[0] // bk, prep, 0)

    qn, qr = _mla_norm_rope(q_ref[...].astype(F32), tabq_ref[...], gqn_ref[...], gqp_ref[...],
                            False)
    qb = jnp.concatenate([qn.astype(BF16), qr.astype(BF16)], axis=-1)
    _softmax_attention(qb, k_scr, vt_scr, o_ref, s_a, s_b, m_scr, l_scr, acc_scr, bq=bq, bk=bk)


def _mla_attention(q_ext, kv, down, tab, gq, gk, *, kp_col, bq=2048, bk=512):
    b, s, _ = q_ext.shape
    h_n = N_MLA_HEADS
    d = QK_NOPE_DIM
    bq = min(bq, s)

    def pack_gain(g):
        g_pe = g[d:]
        half = QK_ROPE_DIM // 2
        g_sw = jnp.concatenate([g_pe[half:], g_pe[:half]])
        return (g[:d].reshape(1, d).astype(F32),
                jnp.concatenate([g_pe, g_sw]).reshape(1, LANES).astype(F32))

    gqn, gqp = pack_gain(gq * (QK_HEAD_DIM ** -0.5 * LOG2E))
    gkn, gkp = pack_gain(gk)
    full = lambda col_fn: pl.BlockSpec((None, s, LANES), lambda bi, h, qi: (bi, 0, col_fn(h)))
    vec = pl.BlockSpec((1, LANES), lambda bi, h, qi: (0, 0))
    return pl.pallas_call(
        functools.partial(_mla_kernel, bq=bq, bk=bk),
        grid=(b, h_n, s // bq),
        in_specs=[pl.BlockSpec((None, bq, 2 * LANES), lambda bi, h, qi: (bi, qi, h)),
                  full(lambda h: 2 * h), full(lambda h: 2 * h + 1),
                  full(lambda h: kp_col), full(lambda h: 0),
                  pl.BlockSpec((None, bq, LANES), lambda bi, h, qi: (bi, qi, 0)),
                  vec, vec, vec, vec],
        out_specs=pl.BlockSpec((None, bq, V_HEAD_DIM), lambda bi, h, qi: (bi, qi, h)),
        out_shape=jax.ShapeDtypeStruct((b, s, h_n * V_HEAD_DIM), BF16),
        scratch_shapes=_attn_scratch(s, bq, bk, 2 * LANES, V_HEAD_DIM),
        compiler_params=_params(3), name="mla_attention")(
            q_ext, kv, kv, down, tab, tab, gqn, gqp, gkn, gkp)


F32_EXP2_ZERO = -150.0


def _sb_kernel(q_ref, k_ref, v_ref, o_ref, vt_scr, r_scr, acc_scr, *, bq, bk, scale):
    assert bq == 2 * bk
    qi = pl.program_id(2)

    @pl.when(qi == 0)
    def _():
        def prep(c, carry):
            _prep_values(v_ref, vt_scr, c, pl.ds(pl.multiple_of(c * bk, bk), bk))
            return carry
        lax.fori_loop(0, k_ref.shape[0] // bk, prep, 0)

    qb = (q_ref[...].astype(F32) * (scale * LOG2E)).astype(BF16)
    row = lax.broadcasted_iota(jnp.int32, (bk, bk), 0)
    col = lax.broadcasted_iota(jnp.int32, (bk, bk), 1)
    later = jnp.where(col > row, 1.0, 0.0).astype(BF16)

    def scan_chunk(c, q0, masked):
        rows = pl.ds(pl.multiple_of(c * bk, bk), bk)
        z = lax.dot_general(k_ref[rows, :], qb[q0:, :], (((1,), (1,)), ((), ())),
                            preferred_element_type=F32)
        log_beta = jnp.minimum(z, 0.0) - jnp.log2(1.0 + jnp.exp2(-jnp.abs(z)))
        log_rest = log_beta - z
        valid = None
        if masked:
            key = lax.broadcasted_iota(jnp.int32, z.shape, 0)
            query = lax.broadcasted_iota(jnp.int32, z.shape, 1)
            valid = key < query
            log_rest = jnp.where(valid, log_rest, 0.0)
        hi = log_rest.astype(BF16)
        lo = (log_rest - hi.astype(F32)).astype(BF16)
        suffix = (jnp.dot(later, hi, preferred_element_type=F32)
                  + jnp.dot(later, lo, preferred_element_type=F32))
        return log_beta + suffix, suffix[0:1, :] + log_rest[0:1, :], valid

    def weights(e, r, valid):
        w = jnp.exp2(e + r)
        if valid is not None:
            w = jnp.where(valid, w, 0.0)
        return w.astype(BF16)

    def window(with_low):
        e_top, t_top, v_top = scan_chunk(2 * qi + 1, bk, True)
        e_mid, t_mid, v_mid = scan_chunk(2 * qi, 0, True)
        if with_low:
            e_low, t_low, _ = scan_chunk(2 * qi - 1, 0, False)
        w_top = weights(e_top, 0.0, v_top)
        r = jnp.concatenate([jnp.zeros((1, bk), F32), t_top], axis=1)
        w_mid = weights(e_mid, r, v_mid)
        r = r + t_mid
        acc = jnp.dot(vt_scr[2 * qi], w_mid, preferred_element_type=F32)
        if with_low:
            w_low = weights(e_low, r, None)
            r = r + t_low
            acc = acc + jnp.dot(vt_scr[2 * qi - 1], w_low, preferred_element_type=F32)
        acc_top = jnp.dot(vt_scr[2 * qi + 1], w_top, preferred_element_type=F32)
        acc_scr[:, :bk] = acc[:, :bk]
        acc_scr[:, bk:] = acc[:, bk:] + acc_top
        r_scr[...] = r

    @pl.when(qi == 0)
    def _():
        window(False)

    @pl.when(qi > 0)
    def _():
        window(True)

    def more(c):
        return jnp.logical_and(c >= 0, jnp.max(r_scr[...]) > F32_EXP2_ZERO)

    def body(c):
        e, t, _ = scan_chunk(c, 0, False)
        r_prev = r_scr[...]
        acc_scr[...] += jnp.dot(vt_scr[c], weights(e, r_prev, None),
                                preferred_element_type=F32)
        r_scr[...] = r_prev + t
        return c - 1

    lax.while_loop(more, body, 2 * qi - 2)
    o_ref[...] = acc_scr[...].T.astype(o_ref.dtype)


def _sb_attention(proj, *, n_heads, q_col, k_col, v_col, bq=512):
    b, s, _ = proj.shape
    d = HEAD_DIM
    bq = min(bq, s)
    bk = bq // 2
    kv_spec = lambda col: pl.BlockSpec((None, s, d), lambda bi, h, qi: (bi, 0, col + h))
    return pl.pallas_call(
        functools.partial(_sb_kernel, bq=bq, bk=bk, scale=d ** -0.5),
        grid=(b, n_heads, s // bq),
        in_specs=[pl.BlockSpec((None, bq, d), lambda bi, h, qi: (bi, qi, q_col + h)),
                  kv_spec(k_col), kv_spec(v_col)],
        out_specs=pl.BlockSpec((None, bq, d), lambda bi, h, qi: (bi, qi, h)),
        out_shape=jax.ShapeDtypeStruct((b, s, n_heads * d), BF16),
        scratch_shapes=[pltpu.VMEM((s // bk, d, bk), BF16), pltpu.VMEM((1, bq), F32),
                        pltpu.VMEM((d, bq), F32)],
        compiler_params=_params(3), name="sb_attention")(proj, proj, proj)


def _rope_kernel(ang_ref, cos_ref, sin_ref):
    ang = ang_ref[...]
    cos_ref[...] = jnp.cos(ang)
    sin_ref[...] = jnp.sin(ang)


def _rope_table(positions):
    b, s = positions.shape
    half = QK_ROPE_DIM // 2
    per_row = LANES // half
    inv_freq = ROPE_THETA ** (-jnp.arange(half, dtype=F32) / half)
    pos = jnp.repeat(positions.astype(F32), half, axis=-1).reshape(b * s // per_row, LANES)
    ang_in = pos * jnp.tile(inv_freq, per_row)[None, :]
    rows = ang_in.shape[0]
    spec = pl.BlockSpec((rows, LANES), lambda i: (0, 0))
    cos, sin = pl.pallas_call(
        _rope_kernel, grid=(1,), in_specs=[spec], out_specs=[spec, spec],
        out_shape=[jax.ShapeDtypeStruct((rows, LANES), F32)] * 2,
        compiler_params=_params(1), name="rope_table")(ang_in)
    cos = cos.reshape(b, s, half)
    sin = sin.reshape(b, s, half)
    return jnp.concatenate([cos, cos, -sin, sin], axis=-1)


def _mlp(x, g, w_up, w_down, layer):
    a = _rms_matmul(x, g, w_up, layer=layer, relu2=True, tn=2048, name="mlp%d_up" % layer)
    return _mm_res([a], w_down, x, layer=layer, tk=4096, name="mlp%d_down" % layer)


def _sb_fox_layer(x, batch, seq, g, w_in, i, b_f, fox_q_g, fox_k_g, w_o):
    d_model = x.shape[1]
    n_heads = d_model // (2 * HEAD_DIM)
    width = n_heads * HEAD_DIM
    w_f = jnp.pad(w_in[i, :, 6 * width:], ((0, 0), (0, LANES - n_heads))).astype(BF16)
    proj, f_logit = _rms_matmul(x, g, w_in, layer=i, n=6 * width, w_aux=w_f, tn=2048,
                                name="in_proj")
    b_pad = jnp.pad(b_f.astype(F32), (0, LANES - n_heads)).reshape(1, LANES)
    f_aug = _forget_cumsum(f_logit, b_pad, seq, n_heads).reshape(batch, seq, LANES)
    proj = proj.reshape(batch, seq, 6 * width)
    o_sb = _sb_attention(proj, n_heads=n_heads, q_col=0, k_col=n_heads, v_col=2 * n_heads)
    o_fx = _fox_attention(proj, f_aug, fox_q_g, fox_k_g, n_heads=n_heads,
                          q_col=3 * n_heads, k_col=4 * n_heads, v_col=5 * n_heads)
    return _mm_res([o_sb.reshape(-1, width), o_fx.reshape(-1, width)], w_o, x, layer=i,
                   name="sf_out_proj")


def _swap_halves(w):
    half = w.shape[-1] // 2
    return jnp.concatenate([w[..., half:], w[..., :half]], axis=-1)


def _mla_layer(x, batch, seq, tab, g, w_down, i, q_a_g, kv_a_g, w_uq, w_ukv, q_g, k_g, w_o):
    lora = Q_LORA_RANK + KV_LORA_RANK
    w_pe = w_down[:, lora:]
    w_down_ext = jnp.concatenate([w_down, _swap_halves(w_pe)], axis=1).astype(BF16)
    down = _rms_matmul(x, g, w_down_ext, out_dtype=F32, tn=w_down_ext.shape[1], name="mla_down")
    w_uq_h = w_uq.reshape(Q_LORA_RANK, N_MLA_HEADS, QK_HEAD_DIM)
    w_uq_ext = jnp.concatenate([w_uq_h, _swap_halves(w_uq_h[..., QK_NOPE_DIM:])], axis=-1)
    w_uq_ext = w_uq_ext.reshape(Q_LORA_RANK, -1).astype(BF16)
    q_ext = _rms_matmul(down, q_a_g, w_uq_ext, xcol=0, tm=2048, tn=2048, name="mla_uq")
    kv = _rms_matmul(down, kv_a_g, w_ukv, layer=i, xcol=1, tm=2048, tn=2048, name="mla_ukv")
    o = _mla_attention(q_ext.reshape(batch, seq, -1), kv.reshape(batch, seq, -1),
                       down.reshape(batch, seq, -1), tab, q_g, k_g, kp_col=lora // LANES)
    return _mm_res([o.reshape(batch * seq, -1)], w_o, x, layer=i, tk=2048, name="mla_out_proj")


def kernel(x, positions, ln_mix_g, ln_mlp_g, sf_w_in, sf_b_f, fox_q_g, fox_k_g, sf_w_o,
           mla_w_down, mla_q_a_g, mla_kv_a_g, mla_w_uq, mla_w_ukv, mla_q_g, mla_k_g,
           mla_w_o, mlp_w_up, mlp_w_down):
    batch, seq, d_model = x.shape
    depth = ln_mix_g.shape[0]
    tab = _rope_table(positions)
    sf_w_in, sf_w_o, mla_w_ukv, mla_w_o, mlp_w_up, mlp_w_down = (
        w.astype(BF16) for w in (sf_w_in, sf_w_o, mla_w_ukv, mla_w_o, mlp_w_up, mlp_w_down))
    h = x.reshape(batch * seq, d_model)
    for layer in range(depth):
        i = layer // 2
        if layer % 2 == 0:
            h = _sb_fox_layer(h, batch, seq, ln_mix_g[layer], sf_w_in, i, sf_b_f[i],
                              fox_q_g[i], fox_k_g[i], sf_w_o)
        else:
            h = _mla_layer(h, batch, seq, tab, ln_mix_g[layer], mla_w_down[i], i, mla_q_a_g[i],
                           mla_kv_a_g[i], mla_w_uq[i], mla_w_ukv, mla_q_g[i], mla_k_g[i],
                           mla_w_o)
        h = _mlp(h, ln_mlp_g[layer], mlp_w_up, mlp_w_down, layer)
    return h.reshape(batch, seq, d_model)
```

```python
import functools

import jax
import jax.numpy as jnp
import numpy as np
from jax import lax
from jax.experimental import pallas as pl
from jax.experimental.pallas import tpu as pltpu

F32 = jnp.float32
BF16 = jnp.bfloat16

HEAD_DIM = 128
N_MLA_HEADS = 16
Q_LORA_RANK = 512
KV_LORA_RANK = 512
QK_NOPE_DIM = 128
QK_ROPE_DIM = 64
QK_HEAD_DIM = QK_NOPE_DIM + QK_ROPE_DIM
V_HEAD_DIM = 128
ROPE_THETA = 10000.0
EPS = 1e-6
LOG2E = 1.4426950408889634

LANES = 128
VMEM_LIMIT = 56 * 1024 * 1024
ARB = "arbitrary"


def _params(n_axes):
    return pltpu.CompilerParams(dimension_semantics=(ARB,) * n_axes,
                                vmem_limit_bytes=VMEM_LIMIT)


def _rms(x, g):
    ms = jnp.mean(x * x, axis=-1, keepdims=True)
    return x * lax.rsqrt(ms + EPS) * g


def _rms_matmul_kernel(x_ref, g_ref, w_ref, *rest, relu2, aux):
    if aux:
        wa_ref, o_ref, oa_ref, h_scr = rest
    else:
        o_ref, h_scr = rest

    @pl.when(pl.program_id(1) == 0)
    def _():
        h = _rms(x_ref[...].astype(F32), g_ref[...]).astype(BF16)
        h_scr[...] = h
        if aux:
            oa_ref[...] = jnp.dot(h, wa_ref[...], preferred_element_type=F32)

    acc = jnp.dot(h_scr[...], w_ref[...].astype(BF16), preferred_element_type=F32)
    if relu2:
        acc = jnp.square(jnp.maximum(acc, 0.0))
    o_ref[...] = acc.astype(o_ref.dtype)


def _weight_spec(w, layer, block, index):
    if w.ndim == 2:
        return pl.BlockSpec(block, index)
    return pl.BlockSpec((None,) + block, lambda *ids: (layer,) + index(*ids))


def _rms_matmul(x, g, w, *, layer=None, n=None, xcol=0, out_dtype=BF16, relu2=False,
                w_aux=None, tm=1024, tn=1024, name):
    t = x.shape[0]
    k = w.shape[-2]
    n = w.shape[-1] if n is None else n
    tm, tn = min(tm, t), min(tn, n)
    assert t % tm == 0 and n % tn == 0
    aux = w_aux is not None
    in_specs = [pl.BlockSpec((tm, k), lambda i, j: (i, xcol)),
                pl.BlockSpec((1, k), lambda i, j: (0, 0)),
                _weight_spec(w, layer, (k, tn), lambda i, j: (0, j))]
    out_specs = pl.BlockSpec((tm, tn), lambda i, j: (i, j))
    out_shape = jax.ShapeDtypeStruct((t, n), out_dtype)
    args = [x, g.reshape(1, k).astype(F32), w]
    if aux:
        na = w_aux.shape[1]
        in_specs.append(pl.BlockSpec((k, na), lambda i, j: (0, 0)))
        out_specs = [out_specs, pl.BlockSpec((tm, na), lambda i, j: (i, 0))]
        out_shape = [out_shape, jax.ShapeDtypeStruct((t, na), F32)]
        args.append(w_aux)
    return pl.pallas_call(
        functools.partial(_rms_matmul_kernel, relu2=relu2, aux=aux),
        grid=(t // tm, n // tn),
        in_specs=in_specs, out_specs=out_specs, out_shape=out_shape,
        scratch_shapes=[pltpu.VMEM((tm, k), BF16)],
        compiler_params=_params(2), name=name)(*args)


def _mm_res_kernel(*refs, n_pairs):
    a_refs = refs[:n_pairs]
    w_refs = refs[n_pairs:2 * n_pairs]
    r_ref, o_ref = refs[2 * n_pairs:]
    k = pl.program_id(2)

    @pl.when(k == 0)
    def _():
        o_ref[...] = r_ref[...]

    acc = jnp.dot(a_refs[0][...], w_refs[0][...].astype(BF16), preferred_element_type=F32)
    for a_ref, w_ref in zip(a_refs[1:], w_refs[1:]):
        acc += jnp.dot(a_ref[...], w_ref[...].astype(BF16), preferred_element_type=F32)
    o_ref[...] += acc


def _mm_res(a_list, w, r, *, layer=None, tm=1024, tn=1024, tk=1024, name):
    t, n = r.shape
    k = a_list[0].shape[1]
    tm, tn, tk = min(tm, t), min(tn, n), min(tk, k)
    assert t % tm == 0 and n % tn == 0 and k % tk == 0
    n_pairs = len(a_list)
    assert w.shape[-2] == n_pairs * k
    k_blocks = k // tk
    in_specs = ([pl.BlockSpec((tm, tk), lambda i, j, kk: (i, kk))] * n_pairs
                + [_weight_spec(w, layer, (tk, tn),
                                lambda i, j, kk, p=p: (p * k_blocks + kk, j))
                   for p in range(n_pairs)]
                + [pl.BlockSpec((tm, tn), lambda i, j, kk: (i, j))])
    return pl.pallas_call(
        functools.partial(_mm_res_kernel, n_pairs=n_pairs),
        grid=(t // tm, n // tn, k // tk),
        in_specs=in_specs,
        out_specs=pl.BlockSpec((tm, tn), lambda i, j, kk: (i, j)),
        out_shape=jax.ShapeDtypeStruct((t, n), F32),
        compiler_params=_params(3), name=name)(*a_list, *([w] * n_pairs), r)


def _log_sigmoid(z):
    return jnp.minimum(z, 0.0) - jnp.log(1.0 + jnp.exp(-jnp.abs(z)))


def _split3(x):
    x1 = x.astype(BF16)
    r1 = x - x1.astype(F32)
    x2 = r1.astype(BF16)
    x3 = (r1 - x2.astype(F32)).astype(BF16)
    return x1, x2, x3


def _forget_cumsum_kernel(f_ref, b_ref, o_ref, *, n_heads, chunk):
    s = f_ref.shape[0]
    row = lax.broadcasted_iota(jnp.int32, (chunk, chunk), 0)
    col = lax.broadcasted_iota(jnp.int32, (chunk, chunk), 1)
    lower = jnp.where(col <= row, 1.0, 0.0).astype(BF16)
    lane = lax.broadcasted_iota(jnp.int32, (chunk, LANES), 1)

    def body(c, carry):
        rows = pl.ds(pl.multiple_of(c * chunk, chunk), chunk)
        lf = jnp.where(lane < n_heads, _log_sigmoid(f_ref[rows, :] + b_ref[...]), 0.0)
        cs = carry
        for part in _split3(lf):
            cs = cs + jnp.dot(lower, part, preferred_element_type=F32)
        hi, mid, lo = _split3(-LOG2E * cs)
        packed = (hi.astype(F32) + pltpu.roll(mid.astype(F32), n_heads, 1)
                  + pltpu.roll(lo.astype(F32), 2 * n_heads, 1))
        o_ref[rows, :] = packed.astype(BF16)
        return cs[chunk - 1:chunk, :]

    lax.fori_loop(0, s // chunk, body, jnp.zeros((1, LANES), F32))


def _forget_cumsum(f_logit, b_pad, seq, n_heads):
    t = f_logit.shape[0]
    return pl.pallas_call(
        functools.partial(_forget_cumsum_kernel, n_heads=n_heads, chunk=min(512, seq)),
        grid=(t // seq,),
        in_specs=[pl.BlockSpec((seq, LANES), lambda b: (b, 0)),
                  pl.BlockSpec((1, LANES), lambda b: (0, 0))],
        out_specs=pl.BlockSpec((seq, LANES), lambda b: (b, 0)),
        out_shape=jax.ShapeDtypeStruct((t, LANES), BF16),
        compiler_params=_params(1), name="forget_cumsum")(f_logit, b_pad)


def _softmax_attention(qb, k_scr, vt_scr, o_ref, s_a, s_b, m_scr, l_scr, acc_scr, *, bq, bk):
    nk = bq // bk
    assert bq == nk * bk and nk % 2 == 0
    qi = pl.program_id(2)
    m_scr[...] = jnp.full(m_scr.shape, -jnp.inf, F32)
    l_scr[...] = jnp.zeros(l_scr.shape, F32)
    acc_scr[...] = jnp.zeros(acc_scr.shape, F32)
    n_full = nk * qi
    bufs = (s_a, s_b)

    def scores(c, q0):
        rows = pl.ds(pl.multiple_of(c * bk, bk), bk)
        return lax.dot_general(k_scr[rows, :], qb[q0:, :], (((1,), (1,)), ((), ())),
                               preferred_element_type=F32)

    def consume(s_ref, c, q0, q1, masked):
        s = s_ref[:, q0:q1]
        if masked:
            key = lax.broadcasted_iota(jnp.int32, s.shape, 0)
            query = lax.broadcasted_iota(jnp.int32, s.shape, 1)
            s = jnp.where(key <= query, s, -jnp.inf)
        m_prev = m_scr[:, q0:q1]
        m_new = jnp.maximum(m_prev, jnp.max(s, axis=0, keepdims=True))
        alpha = jnp.exp2(m_prev - m_new)
        p = jnp.exp2(s - m_new)
        l_scr[:, q0:q1] = alpha * l_scr[:, q0:q1] + jnp.sum(p, axis=0, keepdims=True)
        acc_scr[:, q0:q1] = alpha * acc_scr[:, q0:q1] + jnp.dot(
            vt_scr[c], p.astype(BF16), preferred_element_type=F32)
        m_scr[:, q0:q1] = m_new

    s_a[...] = scores(0, 0)

    def group(i, carry):
        c = nk * i
        for j in range(nk):
            bufs[(j + 1) % 2][...] = scores(c + j + 1, 0)
            consume(bufs[j % 2], c + j, 0, bq, False)
        return carry

    lax.fori_loop(0, qi, group, 0)
    for j in range(nk):
        if j + 1 < nk:
            bufs[(j + 1) % 2][:, (j + 1) * bk:] = scores(n_full + j + 1, (j + 1) * bk)
        consume(bufs[j % 2], n_full + j, j * bk, (j + 1) * bk, True)
        if j + 1 < nk:
            consume(bufs[j % 2], n_full + j, (j + 1) * bk, bq, False)
    o_ref[...] = (acc_scr[...] * (1.0 / l_scr[...])).T.astype(o_ref.dtype)


def _attn_scratch(seq, bq, bk, dk, dv):
    return [pltpu.VMEM((seq, dk), BF16), pltpu.VMEM((seq // bk, dv, bk), BF16),
            pltpu.VMEM((bk, bq), F32), pltpu.VMEM((bk, bq), F32),
            pltpu.VMEM((1, bq), F32), pltpu.VMEM((1, bq), F32), pltpu.VMEM((dv, bq), F32)]


def _lane_sumsq(x, n_valid):
    row = lax.broadcasted_iota(jnp.int32, (x.shape[-1], LANES), 0)
    sel = jnp.where(row < n_valid, 1.0, 0.0).astype(BF16)
    return jnp.dot((x * x).astype(BF16), sel, preferred_element_type=F32)


def _rms_mxu(x, g):
    return x * lax.rsqrt(_lane_sumsq(x, x.shape[-1]) / x.shape[-1] + EPS) * g


def _prep_values(v_ref, vt_scr, c, rows):
    vt_scr[c] = v_ref[rows, :].T


def _fox_kernel(q_ref, k_ref, v_ref, fa_ref, gq_ref, gk_ref, o_ref,
                k_scr, vt_scr, s_a, s_b, m_scr, l_scr, acc_scr, *, bq, bk, n_heads):
    d = HEAD_DIM
    h = pl.program_id(1)

    @pl.when(pl.program_id(2) == 0)
    def _():
        def prep(c, carry):
            rows = pl.ds(pl.multiple_of(c * bk, bk), bk)
            k_scr[rows, :d] = _rms_mxu(k_ref[rows, :].astype(F32), gk_ref[...]).astype(BF16)
            k_scr[rows, d:] = fa_ref[rows, :]
            _prep_values(v_ref, vt_scr, c, rows)
            return carry
        lax.fori_loop(0, k_ref.shape[0] // bk, prep, 0)

    qn = _rms_mxu(q_ref[...].astype(F32), gq_ref[...])
    lane = lax.broadcasted_iota(jnp.int32, (bq, LANES), 1)
    pick = (lane == h) | (lane == h + n_heads) | (lane == h + 2 * n_heads)
    qb = jnp.concatenate([qn.astype(BF16), jnp.where(pick, 1.0, 0.0).astype(BF16)], axis=-1)
    _softmax_attention(qb, k_scr, vt_scr, o_ref, s_a, s_b, m_scr, l_scr, acc_scr, bq=bq, bk=bk)


def _fox_attention(proj, f_aug, gq, gk, *, n_heads, q_col, k_col, v_col, bq=2048, bk=512):
    b, s, _ = proj.shape
    d = HEAD_DIM
    bq = min(bq, s)
    kv_spec = lambda col: pl.BlockSpec((None, s, d), lambda bi, h, qi: (bi, 0, col + h))
    return pl.pallas_call(
        functools.partial(_fox_kernel, bq=bq, bk=bk, n_heads=n_heads),
        grid=(b, n_heads, s // bq),
        in_specs=[pl.BlockSpec((None, bq, d), lambda bi, h, qi: (bi, qi, q_col + h)),
                  kv_spec(k_col), kv_spec(v_col),
                  pl.BlockSpec((None, s, LANES), lambda bi, h, qi: (bi, 0, 0)),
                  pl.BlockSpec((1, d), lambda bi, h, qi: (0, 0)),
                  pl.BlockSpec((1, d), lambda bi, h, qi: (0, 0))],
        out_specs=pl.BlockSpec((None, bq, d), lambda bi, h, qi: (bi, qi, h)),
        out_shape=jax.ShapeDtypeStruct((b, s, n_heads * d), BF16),
        scratch_shapes=_attn_scratch(s, bq, bk, 2 * LANES, d),
        compiler_params=_params(3), name="fox_attention")(
            proj, proj, proj, f_aug, (gq * (d ** -0.5 * LOG2E)).reshape(1, d).astype(F32),
            gk.reshape(1, d).astype(F32))


def _mla_norm_rope(x, tab, g_nope, g_pk, zero_pad):
    nope, pk = x[:, :QK_NOPE_DIM], x[:, QK_NOPE_DIM:]
    r = lax.rsqrt(_lane_sumsq(x, QK_HEAD_DIM) / QK_HEAD_DIM + EPS)
    a = pk * g_pk * tab
    rot = a + pltpu.roll(a, QK_ROPE_DIM, 1)
    if zero_pad:
        lane = lax.broadcasted_iota(jnp.int32, pk.shape, 1)
        rot = jnp.where(lane < QK_ROPE_DIM, rot, 0.0)
    return nope * r * g_nope, rot * r


def _mla_kernel(q_ref, kn_ref, v_ref, kp_ref, tabk_ref, tabq_ref,
                gqn_ref, gqp_ref, gkn_ref, gkp_ref, o_ref,
                k_scr, vt_scr, s_a, s_b, m_scr, l_scr, acc_scr, *, bq, bk):
    d = QK_NOPE_DIM

    @pl.when(pl.program_id(2) == 0)
    def _():
        def prep(c, carry):
            rows = pl.ds(pl.multiple_of(c * bk, bk), bk)
            kx = jnp.concatenate([kn_ref[rows, :].astype(F32), kp_ref[rows, :].astype(F32)],
                                 axis=-1)
            kn, kr = _mla_norm_rope(kx, tabk_ref[rows, :], gkn_ref[...], gkp_ref[...], True)
            k_scr[rows, :d] = kn.astype(BF16)
            k_scr[rows, d:] = kr.astype(BF16)
            _prep_values(v_ref, vt_scr, c, rows)
            return carry
        lax.fori_loop(0, kn_ref.shape[0] // bk, prep, 0)

    qn, qr = _mla_norm_rope(q_ref[...].astype(F32), tabq_ref[...], gqn_ref[...], gqp_ref[...],
                            False)
    qb = jnp.concatenate([qn.astype(BF16), qr.astype(BF16)], axis=-1)
    _softmax_attention(qb, k_scr, vt_scr, o_ref, s_a, s_b, m_scr, l_scr, acc_scr, bq=bq, bk=bk)


def _mla_attention(q_ext, kv, down, tab, gq, gk, *, kp_col, bq=2048, bk=512):
    b, s, _ = q_ext.shape
    h_n = N_MLA_HEADS
    d = QK_NOPE_DIM
    bq = min(bq, s)

    def pack_gain(g):
        g_pe = g[d:]
        half = QK_ROPE_DIM // 2
        g_sw = jnp.concatenate([g_pe[half:], g_pe[:half]])
        return (g[:d].reshape(1, d).astype(F32),
                jnp.concatenate([g_pe, g_sw]).reshape(1, LANES).astype(F32))

    gqn, gqp = pack_gain(gq * (QK_HEAD_DIM ** -0.5 * LOG2E))
    gkn, gkp = pack_gain(gk)
    full = lambda col_fn: pl.BlockSpec((None, s, LANES), lambda bi, h, qi: (bi, 0, col_fn(h)))
    vec = pl.BlockSpec((1, LANES), lambda bi, h, qi: (0, 0))
    return pl.pallas_call(
        functools.partial(_mla_kernel, bq=bq, bk=bk),
        grid=(b, h_n, s // bq),
        in_specs=[pl.BlockSpec((None, bq, 2 * LANES), lambda bi, h, qi: (bi, qi, h)),
                  full(lambda h: 2 * h), full(lambda h: 2 * h + 1),
                  full(lambda h: kp_col), full(lambda h: 0),
                  pl.BlockSpec((None, bq, LANES), lambda bi, h, qi: (bi, qi, 0)),
                  vec, vec, vec, vec],
        out_specs=pl.BlockSpec((None, bq, V_HEAD_DIM), lambda bi, h, qi: (bi, qi, h)),
        out_shape=jax.ShapeDtypeStruct((b, s, h_n * V_HEAD_DIM), BF16),
        scratch_shapes=_attn_scratch(s, bq, bk, 2 * LANES, V_HEAD_DIM),
        compiler_params=_params(3), name="mla_attention")(
            q_ext, kv, kv, down, tab, tab, gqn, gqp, gkn, gkp)


F32_EXP2_ZERO = -150.0


def _sb_kernel(q_ref, k_ref, v_ref, o_ref, vt_scr, r_scr, acc_scr, *, bq, bk, scale):
    assert bq == 2 * bk
    qi = pl.program_id(2)

    @pl.when(qi == 0)
    def _():
        def prep(c, carry):
            _prep_values(v_ref, vt_scr, c, pl.ds(pl.multiple_of(c * bk, bk), bk))
            return carry
        lax.fori_loop(0, k_ref.shape[0] // bk, prep, 0)

    qb = (q_ref[...].astype(F32) * (scale * LOG2E)).astype(BF16)
    row = lax.broadcasted_iota(jnp.int32, (bk, bk), 0)
    col = lax.broadcasted_iota(jnp.int32, (bk, bk), 1)
    later = jnp.where(col > row, 1.0, 0.0).astype(BF16)

    def scan_chunk(c, q0, q1, masked):
        rows = pl.ds(pl.multiple_of(c * bk, bk), bk)
        z = lax.dot_general(k_ref[rows, :], qb[q0:q1, :], (((1,), (1,)), ((), ())),
                            preferred_element_type=F32)
        log_beta = jnp.minimum(z, 0.0) - jnp.log2(1.0 + jnp.exp2(-jnp.abs(z)))
        log_rest = log_beta - z
        valid = None
        if masked:
            key = lax.broadcasted_iota(jnp.int32, z.shape, 0)
            query = lax.broadcasted_iota(jnp.int32, z.shape, 1)
            valid = key < query
            log_rest = jnp.where(valid, log_rest, 0.0)
        hi = log_rest.astype(BF16)
        lo = (log_rest - hi.astype(F32)).astype(BF16)
        suffix = (jnp.dot(later, hi, preferred_element_type=F32)
                  + jnp.dot(later, lo, preferred_element_type=F32))
        return log_beta + suffix, suffix[0:1, :] + log_rest[0:1, :], valid

    def weights(e, r, valid):
        w = jnp.exp2(e + r)
        if valid is not None:
            w = jnp.where(valid, w, 0.0)
        return w.astype(BF16)

    def window(with_low):
        e_top, t_top, v_top = scan_chunk(2 * qi + 1, bk, bq, True)
        e_mid, t_mid, v_mid = scan_chunk(2 * qi, 0, bq, True)
        if with_low:
            e_low, t_low, _ = scan_chunk(2 * qi - 1, 0, bk, False)
        w_top = weights(e_top, 0.0, v_top)
        r = jnp.concatenate([jnp.zeros((1, bk), F32), t_top], axis=1)
        w_mid = weights(e_mid, r, v_mid)
        r = r + t_mid
        acc = jnp.dot(vt_scr[2 * qi], w_mid, preferred_element_type=F32)
        acc_top = jnp.dot(vt_scr[2 * qi + 1], w_top, preferred_element_type=F32)
        acc_lo, r_lo = acc[:, :bk], r[:, :bk]
        if with_low:
            w_low = weights(e_low, r_lo, None)
            r_lo = r_lo + t_low
            acc_lo = acc_lo + jnp.dot(vt_scr[2 * qi - 1], w_low, preferred_element_type=F32)
        acc_scr[:, :bk] = acc_lo
        acc_scr[:, bk:] = acc[:, bk:] + acc_top
        r_scr[:, :bk] = r_lo
        r_scr[:, bk:] = r[:, bk:]

    def visit(c, q0, q1):
        e, t, _ = scan_chunk(c, q0, q1, False)
        r_prev = r_scr[:, q0:q1]
        acc_scr[:, q0:q1] += jnp.dot(vt_scr[c], weights(e, r_prev, None),
                                     preferred_element_type=F32)
        r_scr[:, q0:q1] = r_prev + t

    @pl.when(qi == 0)
    def _():
        window(False)

    @pl.when(qi > 0)
    def _():
        window(True)

    @pl.when(jnp.logical_and(qi > 0, jnp.max(r_scr[:, bk:]) > F32_EXP2_ZERO))
    def _():
        visit(2 * qi - 1, bk, bq)

    def more(c):
        return jnp.logical_and(c >= 0, jnp.max(r_scr[...]) > F32_EXP2_ZERO)

    def body(c):
        visit(c, 0, bq)
        return c - 1

    lax.while_loop(more, body, 2 * qi - 2)
    o_ref[...] = acc_scr[...].T.astype(o_ref.dtype)


def _sb_attention(proj, *, n_heads, q_col, k_col, v_col, bq=512):
    b, s, _ = proj.shape
    d = HEAD_DIM
    bq = min(bq, s)
    bk = bq // 2
    kv_spec = lambda col: pl.BlockSpec((None, s, d), lambda bi, h, qi: (bi, 0, col + h))
    return pl.pallas_call(
        functools.partial(_sb_kernel, bq=bq, bk=bk, scale=d ** -0.5),
        grid=(b, n_heads, s // bq),
        in_specs=[pl.BlockSpec((None, bq, d), lambda bi, h, qi: (bi, qi, q_col + h)),
                  kv_spec(k_col), kv_spec(v_col)],
        out_specs=pl.BlockSpec((None, bq, d), lambda bi, h, qi: (bi, qi, h)),
        out_shape=jax.ShapeDtypeStruct((b, s, n_heads * d), BF16),
        scratch_shapes=[pltpu.VMEM((s // bk, d, bk), BF16), pltpu.VMEM((1, bq), F32),
                        pltpu.VMEM((d, bq), F32)],
        compiler_params=_params(3), name="sb_attention")(proj, proj, proj)


def _rope_kernel(ang_ref, cos_ref, sin_ref):
    ang = ang_ref[...]
    cos_ref[...] = jnp.cos(ang)
    sin_ref[...] = jnp.sin(ang)


def _rope_table(positions):
    b, s = positions.shape
    half = QK_ROPE_DIM // 2
    per_row = LANES // half
    inv_freq = ROPE_THETA ** (-jnp.arange(half, dtype=F32) / half)
    pos = jnp.repeat(positions.astype(F32), half, axis=-1).reshape(b * s // per_row, LANES)
    ang_in = pos * jnp.tile(inv_freq, per_row)[None, :]
    rows = ang_in.shape[0]
    spec = pl.BlockSpec((rows, LANES), lambda i: (0, 0))
    cos, sin = pl.pallas_call(
        _rope_kernel, grid=(1,), in_specs=[spec], out_specs=[spec, spec],
        out_shape=[jax.ShapeDtypeStruct((rows, LANES), F32)] * 2,
        compiler_params=_params(1), name="rope_table")(ang_in)
    cos = cos.reshape(b, s, half)
    sin = sin.reshape(b, s, half)
    return jnp.concatenate([cos, cos, -sin, sin], axis=-1)


def _mlp(x, g, w_up, w_down, layer):
    a = _rms_matmul(x, g, w_up, layer=layer, relu2=True, tn=2048, name="mlp%d_up" % layer)
    return _mm_res([a], w_down, x, layer=layer, tk=4096, name="mlp%d_down" % layer)


def _sb_fox_layer(x, batch, seq, g, w_in, i, b_f, fox_q_g, fox_k_g, w_o):
    d_model = x.shape[1]
    n_heads = d_model // (2 * HEAD_DIM)
    width = n_heads * HEAD_DIM
    w_f = jnp.pad(w_in[i, :, 6 * width:], ((0, 0), (0, LANES - n_heads))).astype(BF16)
    proj, f_logit = _rms_matmul(x, g, w_in, layer=i, n=6 * width, w_aux=w_f, tn=2048,
                                name="in_proj")
    b_pad = jnp.pad(b_f.astype(F32), (0, LANES - n_heads)).reshape(1, LANES)
    f_aug = _forget_cumsum(f_logit, b_pad, seq, n_heads).reshape(batch, seq, LANES)
    proj = proj.reshape(batch, seq, 6 * width)
    o_sb = _sb_attention(proj, n_heads=n_heads, q_col=0, k_col=n_heads, v_col=2 * n_heads)
    o_fx = _fox_attention(proj, f_aug, fox_q_g, fox_k_g, n_heads=n_heads,
                          q_col=3 * n_heads, k_col=4 * n_heads, v_col=5 * n_heads)
    return _mm_res([o_sb.reshape(-1, width), o_fx.reshape(-1, width)], w_o, x, layer=i,
                   name="sf_out_proj")


def _swap_halves(w):
    half = w.shape[-1] // 2
    return jnp.concatenate([w[..., half:], w[..., :half]], axis=-1)


def _mla_layer(x, batch, seq, tab, g, w_down, i, q_a_g, kv_a_g, w_uq, w_ukv, q_g, k_g, w_o):
    lora = Q_LORA_RANK + KV_LORA_RANK
    w_pe = w_down[:, lora:]
    w_down_ext = jnp.concatenate([w_down, _swap_halves(w_pe)], axis=1).astype(BF16)
    down = _rms_matmul(x, g, w_down_ext, out_dtype=F32, tn=w_down_ext.shape[1], name="mla_down")
    w_uq_h = w_uq.reshape(Q_LORA_RANK, N_MLA_HEADS, QK_HEAD_DIM)
    w_uq_ext = jnp.concatenate([w_uq_h, _swap_halves(w_uq_h[..., QK_NOPE_DIM:])], axis=-1)
    w_uq_ext = w_uq_ext.reshape(Q_LORA_RANK, -1).astype(BF16)
    q_ext = _rms_matmul(down, q_a_g, w_uq_ext, xcol=0, tm=2048, tn=2048, name="mla_uq")
    kv = _rms_matmul(down, kv_a_g, w_ukv, layer=i, xcol=1, tm=2048, tn=2048, name="mla_ukv")
    o = _mla_attention(q_ext.reshape(batch, seq, -1), kv.reshape(batch, seq, -1),
                       down.reshape(batch, seq, -1), tab, q_g, k_g, kp_col=lora // LANES)
    return _mm_res([o.reshape(batch * seq, -1)], w_o, x, layer=i, tk=2048, name="mla_out_proj")


def kernel(x, positions, ln_mix_g, ln_mlp_g, sf_w_in, sf_b_f, fox_q_g, fox_k_g, sf_w_o,
           mla_w_down, mla_q_a_g, mla_kv_a_g, mla_w_uq, mla_w_ukv, mla_q_g, mla_k_g,
           mla_w_o, mlp_w_up, mlp_w_down):
    batch, seq, d_model = x.shape
    depth = ln_mix_g.shape[0]
    tab = _rope_table(positions)
    sf_w_in, sf_w_o, mla_w_ukv, mla_w_o, mlp_w_up, mlp_w_down = (
        w.astype(BF16) for w in (sf_w_in, sf_w_o, mla_w_ukv, mla_w_o, mlp_w_up, mlp_w_down))
    h = x.reshape(batch * seq, d_model)
    for layer in range(depth):
        i = layer // 2
        if layer % 2 == 0:
            h = _sb_fox_layer(h, batch, seq, ln_mix_g[layer], sf_w_in, i, sf_b_f[i],
                              fox_q_g[i], fox_k_g[i], sf_w_o)
        else:
            h = _mla_layer(h, batch, seq, tab, ln_mix_g[layer], mla_w_down[i], i, mla_q_a_g[i],
                           mla_kv_a_g[i], mla_w_uq[i], mla_w_ukv, mla_q_g[i], mla_k_g[i],
                           mla_w_o)
        h = _mlp(h, ln_mlp_g[layer], mlp_w_up, mlp_w_down, layer)
    return h.reshape(batch, seq, d_model)
```

```python
import functools

import jax
import jax.numpy as jnp
import numpy as np
from jax import lax
from jax.experimental import pallas as pl
from jax.experimental.pallas import tpu as pltpu

F32 = jnp.float32
BF16 = jnp.bfloat16

HEAD_DIM = 128
N_MLA_HEADS = 16
Q_LORA_RANK = 512
KV_LORA_RANK = 512
QK_NOPE_DIM = 128
QK_ROPE_DIM = 64
QK_HEAD_DIM = QK_NOPE_DIM + QK_ROPE_DIM
V_HEAD_DIM = 128
ROPE_THETA = 10000.0
EPS = 1e-6
LOG2E = 1.4426950408889634

LANES = 128
VMEM_LIMIT = 56 * 1024 * 1024
ARB = "arbitrary"


def _params(n_axes):
    return pltpu.CompilerParams(dimension_semantics=(ARB,) * n_axes,
                                vmem_limit_bytes=VMEM_LIMIT)


def _rms(x, g):
    ms = jnp.mean(x * x, axis=-1, keepdims=True)
    return x * lax.rsqrt(ms + EPS) * g


def _rms_matmul_kernel(x_ref, g_ref, w_ref, *rest, relu2, aux):
    if aux:
        wa_ref, o_ref, oa_ref, h_scr = rest
    else:
        o_ref, h_scr = rest

    @pl.when(pl.program_id(1) == 0)
    def _():
        h = _rms(x_ref[...].astype(F32), g_ref[...]).astype(BF16)
        h_scr[...] = h
        if aux:
            oa_ref[...] = jnp.dot(h, wa_ref[...], preferred_element_type=F32)

    acc = jnp.dot(h_scr[...], w_ref[...].astype(BF16), preferred_element_type=F32)
    if relu2:
        acc = jnp.square(jnp.maximum(acc, 0.0))
    o_ref[...] = acc.astype(o_ref.dtype)


def _weight_spec(w, layer, block, index):
    if w.ndim == 2:
        return pl.BlockSpec(block, index)
    return pl.BlockSpec((None,) + block, lambda *ids: (layer,) + index(*ids))


def _rms_matmul(x, g, w, *, layer=None, n=None, xcol=0, out_dtype=BF16, relu2=False,
                w_aux=None, tm=1024, tn=1024, name):
    t = x.shape[0]
    k = w.shape[-2]
    n = w.shape[-1] if n is None else n
    tm, tn = min(tm, t), min(tn, n)
    assert t % tm == 0 and n % tn == 0
    aux = w_aux is not None
    in_specs = [pl.BlockSpec((tm, k), lambda i, j: (i, xcol)),
                pl.BlockSpec((1, k), lambda i, j: (0, 0)),
                _weight_spec(w, layer, (k, tn), lambda i, j: (0, j))]
    out_specs = pl.BlockSpec((tm, tn), lambda i, j: (i, j))
    out_shape = jax.ShapeDtypeStruct((t, n), out_dtype)
    args = [x, g.reshape(1, k).astype(F32), w]
    if aux:
        na = w_aux.shape[1]
        in_specs.append(pl.BlockSpec((k, na), lambda i, j: (0, 0)))
        out_specs = [out_specs, pl.BlockSpec((tm, na), lambda i, j: (i, 0))]
        out_shape = [out_shape, jax.ShapeDtypeStruct((t, na), F32)]
        args.append(w_aux)
    return pl.pallas_call(
        functools.partial(_rms_matmul_kernel, relu2=relu2, aux=aux),
        grid=(t // tm, n // tn),
        in_specs=in_specs, out_specs=out_specs, out_shape=out_shape,
        scratch_shapes=[pltpu.VMEM((tm, k), BF16)],
        compiler_params=_params(2), name=name)(*args)


def _mm_res_kernel(*refs, n_pairs):
    a_refs = refs[:n_pairs]
    w_refs = refs[n_pairs:2 * n_pairs]
    r_ref, o_ref = refs[2 * n_pairs:]
    k = pl.program_id(2)

    @pl.when(k == 0)
    def _():
        o_ref[...] = r_ref[...]

    acc = jnp.dot(a_refs[0][...], w_refs[0][...].astype(BF16), preferred_element_type=F32)
    for a_ref, w_ref in zip(a_refs[1:], w_refs[1:]):
        acc += jnp.dot(a_ref[...], w_ref[...].astype(BF16), preferred_element_type=F32)
    o_ref[...] += acc


def _mm_res(a_list, w, r, *, layer=None, tm=1024, tn=1024, tk=1024, name):
    t, n = r.shape
    k = a_list[0].shape[1]
    tm, tn, tk = min(tm, t), min(tn, n), min(tk, k)
    assert t % tm == 0 and n % tn == 0 and k % tk == 0
    n_pairs = len(a_list)
    assert w.shape[-2] == n_pairs * k
    k_blocks = k // tk
    in_specs = ([pl.BlockSpec((tm, tk), lambda i, j, kk: (i, kk))] * n_pairs
                + [_weight_spec(w, layer, (tk, tn),
                                lambda i, j, kk, p=p: (p * k_blocks + kk, j))
                   for p in range(n_pairs)]
                + [pl.BlockSpec((tm, tn), lambda i, j, kk: (i, j))])
    return pl.pallas_call(
        functools.partial(_mm_res_kernel, n_pairs=n_pairs),
        grid=(t // tm, n // tn, k // tk),
        in_specs=in_specs,
        out_specs=pl.BlockSpec((tm, tn), lambda i, j, kk: (i, j)),
        out_shape=jax.ShapeDtypeStruct((t, n), F32),
        compiler_params=_params(3), name=name)(*a_list, *([w] * n_pairs), r)


def _log_sigmoid(z):
    return jnp.minimum(z, 0.0) - jnp.log(1.0 + jnp.exp(-jnp.abs(z)))


def _split3(x):
    x1 = x.astype(BF16)
    r1 = x - x1.astype(F32)
    x2 = r1.astype(BF16)
    x3 = (r1 - x2.astype(F32)).astype(BF16)
    return x1, x2, x3


def _forget_cumsum_kernel(f_ref, b_ref, o_ref, *, n_heads, chunk):
    s = f_ref.shape[0]
    row = lax.broadcasted_iota(jnp.int32, (chunk, chunk), 0)
    col = lax.broadcasted_iota(jnp.int32, (chunk, chunk), 1)
    lower = jnp.where(col <= row, 1.0, 0.0).astype(BF16)
    lane = lax.broadcasted_iota(jnp.int32, (chunk, LANES), 1)

    def body(c, carry):
        rows = pl.ds(pl.multiple_of(c * chunk, chunk), chunk)
        lf = jnp.where(lane < n_heads, _log_sigmoid(f_ref[rows, :] + b_ref[...]), 0.0)
        cs = carry
        for part in _split3(lf):
            cs = cs + jnp.dot(lower, part, preferred_element_type=F32)
        hi, mid, lo = _split3(-LOG2E * cs)
        packed = (hi.astype(F32) + pltpu.roll(mid.astype(F32), n_heads, 1)
                  + pltpu.roll(lo.astype(F32), 2 * n_heads, 1))
        o_ref[rows, :] = packed.astype(BF16)
        return cs[chunk - 1:chunk, :]

    lax.fori_loop(0, s // chunk, body, jnp.zeros((1, LANES), F32))


def _forget_cumsum(f_logit, b_pad, seq, n_heads):
    t = f_logit.shape[0]
    return pl.pallas_call(
        functools.partial(_forget_cumsum_kernel, n_heads=n_heads, chunk=min(512, seq)),
        grid=(t // seq,),
        in_specs=[pl.BlockSpec((seq, LANES), lambda b: (b, 0)),
                  pl.BlockSpec((1, LANES), lambda b: (0, 0))],
        out_specs=pl.BlockSpec((seq, LANES), lambda b: (b, 0)),
        out_shape=jax.ShapeDtypeStruct((t, LANES), BF16),
        compiler_params=_params(1), name="forget_cumsum")(f_logit, b_pad)


def _softmax_attention(qb, k_scr, vt_scr, o_ref, s_a, s_b, m_scr, l_scr, acc_scr, *, bq, bk):
    nk = bq // bk
    assert bq == nk * bk and nk % 2 == 0
    qi = pl.program_id(2)
    m_scr[...] = jnp.full(m_scr.shape, -jnp.inf, F32)
    l_scr[...] = jnp.zeros(l_scr.shape, F32)
    acc_scr[...] = jnp.zeros(acc_scr.shape, F32)
    n_full = nk * qi
    bufs = (s_a, s_b)

    def scores(c, q0):
        rows = pl.ds(pl.multiple_of(c * bk, bk), bk)
        return lax.dot_general(k_scr[rows, :], qb[q0:, :], (((1,), (1,)), ((), ())),
                               preferred_element_type=F32)

    def consume(s_ref, c, q0, q1, masked):
        s = s_ref[:, q0:q1]
        if masked:
            key = lax.broadcasted_iota(jnp.int32, s.shape, 0)
            query = lax.broadcasted_iota(jnp.int32, s.shape, 1)
            s = jnp.where(key <= query, s, -jnp.inf)
        m_prev = m_scr[:, q0:q1]
        m_new = jnp.maximum(m_prev, jnp.max(s, axis=0, keepdims=True))
        alpha = jnp.exp2(m_prev - m_new)
        p = jnp.exp2(s - m_new)
        l_scr[:, q0:q1] = alpha * l_scr[:, q0:q1] + jnp.sum(p, axis=0, keepdims=True)
        acc_scr[:, q0:q1] = alpha * acc_scr[:, q0:q1] + jnp.dot(
            vt_scr[c], p.astype(BF16), preferred_element_type=F32)
        m_scr[:, q0:q1] = m_new

    s_a[...] = scores(0, 0)

    def group(i, carry):
        c = nk * i
        for j in range(nk):
            bufs[(j + 1) % 2][...] = scores(c + j + 1, 0)
            consume(bufs[j % 2], c + j, 0, bq, False)
        return carry

    lax.fori_loop(0, qi, group, 0)
    for j in range(nk):
        if j + 1 < nk:
            bufs[(j + 1) % 2][:, (j + 1) * bk:] = scores(n_full + j + 1, (j + 1) * bk)
        consume(bufs[j % 2], n_full + j, j * bk, (j + 1) * bk, True)
        if j + 1 < nk:
            consume(bufs[j % 2], n_full + j, (j + 1) * bk, bq, False)
    o_ref[...] = (acc_scr[...] * (1.0 / l_scr[...])).T.astype(o_ref.dtype)


def _attn_scratch(seq, bq, bk, dk, dv):
    return [pltpu.VMEM((seq, dk), BF16), pltpu.VMEM((seq // bk, dv, bk), BF16),
            pltpu.VMEM((bk, bq), F32), pltpu.VMEM((bk, bq), F32),
            pltpu.VMEM((1, bq), F32), pltpu.VMEM((1, bq), F32), pltpu.VMEM((dv, bq), F32)]


def _lane_sumsq(x, n_valid):
    row = lax.broadcasted_iota(jnp.int32, (x.shape[-1], LANES), 0)
    sel = jnp.where(row < n_valid, 1.0, 0.0).astype(BF16)
    return jnp.dot((x * x).astype(BF16), sel, preferred_element_type=F32)


def _rms_mxu(x, g):
    return x * lax.rsqrt(_lane_sumsq(x, x.shape[-1]) / x.shape[-1] + EPS) * g


def _cast_rider(w, grid):
    n_layers, n_rows, n_cols = w.shape
    steps = grid[0] * grid[1] * grid[2]
    rows = n_layers * n_rows // steps
    per_layer = n_rows // rows
    assert rows * steps == n_layers * n_rows and rows % 16 == 0 and n_rows % rows == 0

    def index(a, b, c):
        step = (a * grid[1] + b) * grid[2] + c
        return step // per_layer, step % per_layer, 0

    return pl.BlockSpec((None, rows, n_cols), index), jax.ShapeDtypeStruct(w.shape, BF16)


def _prep_values(v_ref, vt_scr, c, rows):
    vt_scr[c] = v_ref[rows, :].T


def _fox_kernel(q_ref, k_ref, v_ref, fa_ref, gq_ref, gk_ref, w32_ref, o_ref, w16_ref,
                k_scr, vt_scr, s_a, s_b, m_scr, l_scr, acc_scr, *, bq, bk, n_heads):
    d = HEAD_DIM
    h = pl.program_id(1)
    w16_ref[...] = w32_ref[...].astype(BF16)

    @pl.when(pl.program_id(2) == 0)
    def _():
        def prep(c, carry):
            rows = pl.ds(pl.multiple_of(c * bk, bk), bk)
            k_scr[rows, :d] = _rms_mxu(k_ref[rows, :].astype(F32), gk_ref[...]).astype(BF16)
            k_scr[rows, d:] = fa_ref[rows, :]
            _prep_values(v_ref, vt_scr, c, rows)
            return carry
        lax.fori_loop(0, k_ref.shape[0] // bk, prep, 0)

    qn = _rms_mxu(q_ref[...].astype(F32), gq_ref[...])
    lane = lax.broadcasted_iota(jnp.int32, (bq, LANES), 1)
    pick = (lane == h) | (lane == h + n_heads) | (lane == h + 2 * n_heads)
    qb = jnp.concatenate([qn.astype(BF16), jnp.where(pick, 1.0, 0.0).astype(BF16)], axis=-1)
    _softmax_attention(qb, k_scr, vt_scr, o_ref, s_a, s_b, m_scr, l_scr, acc_scr, bq=bq, bk=bk)


def _fox_attention(proj, f_aug, gq, gk, w32, *, n_heads, q_col, k_col, v_col, bq=2048, bk=512):
    b, s, _ = proj.shape
    d = HEAD_DIM
    bq = min(bq, s)
    grid = (b, n_heads, s // bq)
    if w32 is None:
        w32 = jnp.zeros((1, 16 * grid[0] * grid[1] * grid[2], LANES), F32)
    w_spec, w_shape = _cast_rider(w32, grid)
    kv_spec = lambda col: pl.BlockSpec((None, s, d), lambda bi, h, qi: (bi, 0, col + h))
    return pl.pallas_call(
        functools.partial(_fox_kernel, bq=bq, bk=bk, n_heads=n_heads),
        grid=grid,
        in_specs=[pl.BlockSpec((None, bq, d), lambda bi, h, qi: (bi, qi, q_col + h)),
                  kv_spec(k_col), kv_spec(v_col),
                  pl.BlockSpec((None, s, LANES), lambda bi, h, qi: (bi, 0, 0)),
                  pl.BlockSpec((1, d), lambda bi, h, qi: (0, 0)),
                  pl.BlockSpec((1, d), lambda bi, h, qi: (0, 0)), w_spec],
        out_specs=[pl.BlockSpec((None, bq, d), lambda bi, h, qi: (bi, qi, h)), w_spec],
        out_shape=[jax.ShapeDtypeStruct((b, s, n_heads * d), BF16), w_shape],
        scratch_shapes=_attn_scratch(s, bq, bk, 2 * LANES, d),
        compiler_params=_params(3), name="fox_attention")(
            proj, proj, proj, f_aug, (gq * (d ** -0.5 * LOG2E)).reshape(1, d).astype(F32),
            gk.reshape(1, d).astype(F32), w32)


def _mla_norm_rope(x, tab, g_nope, g_pk, zero_pad):
    nope, pk = x[:, :QK_NOPE_DIM], x[:, QK_NOPE_DIM:]
    r = lax.rsqrt(_lane_sumsq(x, QK_HEAD_DIM) / QK_HEAD_DIM + EPS)
    a = pk * g_pk * tab
    rot = a + pltpu.roll(a, QK_ROPE_DIM, 1)
    if zero_pad:
        lane = lax.broadcasted_iota(jnp.int32, pk.shape, 1)
        rot = jnp.where(lane < QK_ROPE_DIM, rot, 0.0)
    return nope * r * g_nope, rot * r


def _mla_kernel(q_ref, kn_ref, v_ref, kp_ref, tabk_ref, tabq_ref,
                gqn_ref, gqp_ref, gkn_ref, gkp_ref, o_ref,
                k_scr, vt_scr, s_a, s_b, m_scr, l_scr, acc_scr, *, bq, bk):
    d = QK_NOPE_DIM

    @pl.when(pl.program_id(2) == 0)
    def _():
        def prep(c, carry):
            rows = pl.ds(pl.multiple_of(c * bk, bk), bk)
            kx = jnp.concatenate([kn_ref[rows, :].astype(F32), kp_ref[rows, :].astype(F32)],
                                 axis=-1)
            kn, kr = _mla_norm_rope(kx, tabk_ref[rows, :], gkn_ref[...], gkp_ref[...], True)
            k_scr[rows, :d] = kn.astype(BF16)
            k_scr[rows, d:] = kr.astype(BF16)
            _prep_values(v_ref, vt_scr, c, rows)
            return carry
        lax.fori_loop(0, kn_ref.shape[0] // bk, prep, 0)

    qn, qr = _mla_norm_rope(q_ref[...].astype(F32), tabq_ref[...], gqn_ref[...], gqp_ref[...],
                            False)
    qb = jnp.concatenate([qn.astype(BF16), qr.astype(BF16)], axis=-1)
    _softmax_attention(qb, k_scr, vt_scr, o_ref, s_a, s_b, m_scr, l_scr, acc_scr, bq=bq, bk=bk)


def _mla_attention(q_ext, kv, down, tab, gq, gk, *, kp_col, bq=2048, bk=512):
    b, s, _ = q_ext.shape
    h_n = N_MLA_HEADS
    d = QK_NOPE_DIM
    bq = min(bq, s)

    def pack_gain(g):
        g_pe = g[d:]
        half = QK_ROPE_DIM // 2
        g_sw = jnp.concatenate([g_pe[half:], g_pe[:half]])
        return (g[:d].reshape(1, d).astype(F32),
                jnp.concatenate([g_pe, g_sw]).reshape(1, LANES).astype(F32))

    gqn, gqp = pack_gain(gq * (QK_HEAD_DIM ** -0.5 * LOG2E))
    gkn, gkp = pack_gain(gk)
    full = lambda col_fn: pl.BlockSpec((None, s, LANES), lambda bi, h, qi: (bi, 0, col_fn(h)))
    vec = pl.BlockSpec((1, LANES), lambda bi, h, qi: (0, 0))
    return pl.pallas_call(
        functools.partial(_mla_kernel, bq=bq, bk=bk),
        grid=(b, h_n, s // bq),
        in_specs=[pl.BlockSpec((None, bq, 2 * LANES), lambda bi, h, qi: (bi, qi, h)),
                  full(lambda h: 2 * h), full(lambda h: 2 * h + 1),
                  full(lambda h: kp_col), full(lambda h: 0),
                  pl.BlockSpec((None, bq, LANES), lambda bi, h, qi: (bi, qi, 0)),
                  vec, vec, vec, vec],
        out_specs=pl.BlockSpec((None, bq, V_HEAD_DIM), lambda bi, h, qi: (bi, qi, h)),
        out_shape=jax.ShapeDtypeStruct((b, s, h_n * V_HEAD_DIM), BF16),
        scratch_shapes=_attn_scratch(s, bq, bk, 2 * LANES, V_HEAD_DIM),
        compiler_params=_params(3), name="mla_attention")(
            q_ext, kv, kv, down, tab, tab, gqn, gqp, gkn, gkp)


F32_EXP2_ZERO = -150.0


def _sb_kernel(q_ref, k_ref, v_ref, w32_ref, o_ref, w16_ref, vt_scr, r_scr, acc_scr, *,
               bq, bk, scale):
    assert bq == 2 * bk
    qi = pl.program_id(2)
    w16_ref[...] = w32_ref[...].astype(BF16)

    @pl.when(qi == 0)
    def _():
        def prep(c, carry):
            _prep_values(v_ref, vt_scr, c, pl.ds(pl.multiple_of(c * bk, bk), bk))
            return carry
        lax.fori_loop(0, k_ref.shape[0] // bk, prep, 0)

    qb = (q_ref[...].astype(F32) * (scale * LOG2E)).astype(BF16)
    row = lax.broadcasted_iota(jnp.int32, (bk, bk), 0)
    col = lax.broadcasted_iota(jnp.int32, (bk, bk), 1)
    later = jnp.where(col > row, 1.0, 0.0).astype(BF16)

    def scan_chunk(c, q0, q1, masked):
        rows = pl.ds(pl.multiple_of(c * bk, bk), bk)
        z = lax.dot_general(k_ref[rows, :], qb[q0:q1, :], (((1,), (1,)), ((), ())),
                            preferred_element_type=F32)
        log_beta = jnp.minimum(z, 0.0) - jnp.log2(1.0 + jnp.exp2(-jnp.abs(z)))
        log_rest = log_beta - z
        valid = None
        if masked:
            key = lax.broadcasted_iota(jnp.int32, z.shape, 0)
            query = lax.broadcasted_iota(jnp.int32, z.shape, 1)
            valid = key < query
            log_rest = jnp.where(valid, log_rest, 0.0)
        hi = log_rest.astype(BF16)
        lo = (log_rest - hi.astype(F32)).astype(BF16)
        suffix = (jnp.dot(later, hi, preferred_element_type=F32)
                  + jnp.dot(later, lo, preferred_element_type=F32))
        return log_beta + suffix, suffix[0:1, :] + log_rest[0:1, :], valid

    def weights(e, r, valid):
        w = jnp.exp2(e + r)
        if valid is not None:
            w = jnp.where(valid, w, 0.0)
        return w.astype(BF16)

    def window(with_low):
        e_top, t_top, v_top = scan_chunk(2 * qi + 1, bk, bq, True)
        e_mid, t_mid, v_mid = scan_chunk(2 * qi, 0, bq, True)
        if with_low:
            e_low, t_low, _ = scan_chunk(2 * qi - 1, 0, bk, False)
        w_top = weights(e_top, 0.0, v_top)
        r = jnp.concatenate([jnp.zeros((1, bk), F32), t_top], axis=1)
        w_mid = weights(e_mid, r, v_mid)
        r = r + t_mid
        acc = jnp.dot(vt_scr[2 * qi], w_mid, preferred_element_type=F32)
        acc_top = jnp.dot(vt_scr[2 * qi + 1], w_top, preferred_element_type=F32)
        acc_lo, r_lo = acc[:, :bk], r[:, :bk]
        if with_low:
            w_low = weights(e_low, r_lo, None)
            r_lo = r_lo + t_low
            acc_lo = acc_lo + jnp.dot(vt_scr[2 * qi - 1], w_low, preferred_element_type=F32)
        acc_scr[:, :bk] = acc_lo
        acc_scr[:, bk:] = acc[:, bk:] + acc_top
        r_scr[:, :bk] = r_lo
        r_scr[:, bk:] = r[:, bk:]

    def visit(c, q0, q1):
        e, t, _ = scan_chunk(c, q0, q1, False)
        r_prev = r_scr[:, q0:q1]
        acc_scr[:, q0:q1] += jnp.dot(vt_scr[c], weights(e, r_prev, None),
                                     preferred_element_type=F32)
        r_scr[:, q0:q1] = r_prev + t

    @pl.when(qi == 0)
    def _():
        window(False)

    @pl.when(qi > 0)
    def _():
        window(True)

    @pl.when(jnp.logical_and(qi > 0, jnp.max(r_scr[:, bk:]) > F32_EXP2_ZERO))
    def _():
        visit(2 * qi - 1, bk, bq)

    def more(c):
        return jnp.logical_and(c >= 0, jnp.max(r_scr[...]) > F32_EXP2_ZERO)

    def body(c):
        visit(c, 0, bq)
        return c - 1

    lax.while_loop(more, body, 2 * qi - 2)
    o_ref[...] = acc_scr[...].T.astype(o_ref.dtype)


def _sb_attention(proj, w32, *, n_heads, q_col, k_col, v_col, bq=512):
    b, s, _ = proj.shape
    d = HEAD_DIM
    bq = min(bq, s)
    bk = bq // 2
    grid = (b, n_heads, s // bq)
    if w32 is None:
        w32 = jnp.zeros((1, 16 * grid[0] * grid[1] * grid[2], LANES), F32)
    w_spec, w_shape = _cast_rider(w32, grid)
    kv_spec = lambda col: pl.BlockSpec((None, s, d), lambda bi, h, qi: (bi, 0, col + h))
    return pl.pallas_call(
        functools.partial(_sb_kernel, bq=bq, bk=bk, scale=d ** -0.5),
        grid=grid,
        in_specs=[pl.BlockSpec((None, bq, d), lambda bi, h, qi: (bi, qi, q_col + h)),
                  kv_spec(k_col), kv_spec(v_col), w_spec],
        out_specs=[pl.BlockSpec((None, bq, d), lambda bi, h, qi: (bi, qi, h)), w_spec],
        out_shape=[jax.ShapeDtypeStruct((b, s, n_heads * d), BF16), w_shape],
        scratch_shapes=[pltpu.VMEM((s // bk, d, bk), BF16), pltpu.VMEM((1, bq), F32),
                        pltpu.VMEM((d, bq), F32)],
        compiler_params=_params(3), name="sb_attention")(proj, proj, proj, w32)


def _rope_kernel(ang_ref, cos_ref, sin_ref):
    ang = ang_ref[...]
    cos_ref[...] = jnp.cos(ang)
    sin_ref[...] = jnp.sin(ang)


def _rope_table(positions):
    b, s = positions.shape
    half = QK_ROPE_DIM // 2
    per_row = LANES // half
    inv_freq = ROPE_THETA ** (-jnp.arange(half, dtype=F32) / half)
    pos = jnp.repeat(positions.astype(F32), half, axis=-1).reshape(b * s // per_row, LANES)
    ang_in = pos * jnp.tile(inv_freq, per_row)[None, :]
    rows = ang_in.shape[0]
    spec = pl.BlockSpec((rows, LANES), lambda i: (0, 0))
    cos, sin = pl.pallas_call(
        _rope_kernel, grid=(1,), in_specs=[spec], out_specs=[spec, spec],
        out_shape=[jax.ShapeDtypeStruct((rows, LANES), F32)] * 2,
        compiler_params=_params(1), name="rope_table")(ang_in)
    cos = cos.reshape(b, s, half)
    sin = sin.reshape(b, s, half)
    return jnp.concatenate([cos, cos, -sin, sin], axis=-1)


def _mlp(x, g, w_up, w_down, layer):
    a = _rms_matmul(x, g, w_up, layer=layer, relu2=True, tn=2048, name="mlp%d_up" % layer)
    return _mm_res([a], w_down, x, layer=layer, tk=4096, name="mlp%d_down" % layer)


def _sb_fox_layer(x, batch, seq, g, w_in, i, b_f, fox_q_g, fox_k_g, w_o, riders):
    d_model = x.shape[1]
    n_heads = d_model // (2 * HEAD_DIM)
    width = n_heads * HEAD_DIM
    w_f = jnp.pad(w_in[i, :, 6 * width:], ((0, 0), (0, LANES - n_heads))).astype(BF16)
    proj, f_logit = _rms_matmul(x, g, w_in, layer=i, n=6 * width, w_aux=w_f, tn=2048,
                                name="in_proj")
    b_pad = jnp.pad(b_f.astype(F32), (0, LANES - n_heads)).reshape(1, LANES)
    f_aug = _forget_cumsum(f_logit, b_pad, seq, n_heads).reshape(batch, seq, LANES)
    proj = proj.reshape(batch, seq, 6 * width)
    o_sb, cast_a = _sb_attention(proj, riders[0], n_heads=n_heads, q_col=0, k_col=n_heads,
                                 v_col=2 * n_heads)
    o_fx, cast_b = _fox_attention(proj, f_aug, fox_q_g, fox_k_g, riders[1], n_heads=n_heads,
                                  q_col=3 * n_heads, k_col=4 * n_heads, v_col=5 * n_heads)
    out = _mm_res([o_sb.reshape(-1, width), o_fx.reshape(-1, width)], w_o, x, layer=i,
                  name="sf_out_proj")
    return out, (cast_a, cast_b)


def _swap_halves(w):
    half = w.shape[-1] // 2
    return jnp.concatenate([w[..., half:], w[..., :half]], axis=-1)


def _mla_layer(x, batch, seq, tab, g, w_down, i, q_a_g, kv_a_g, w_uq, w_ukv, q_g, k_g, w_o):
    lora = Q_LORA_RANK + KV_LORA_RANK
    w_pe = w_down[:, lora:]
    w_down_ext = jnp.concatenate([w_down, _swap_halves(w_pe)], axis=1).astype(BF16)
    down = _rms_matmul(x, g, w_down_ext, out_dtype=F32, tn=w_down_ext.shape[1], name="mla_down")
    w_uq_h = w_uq.reshape(Q_LORA_RANK, N_MLA_HEADS, QK_HEAD_DIM)
    w_uq_ext = jnp.concatenate([w_uq_h, _swap_halves(w_uq_h[..., QK_NOPE_DIM:])], axis=-1)
    w_uq_ext = w_uq_ext.reshape(Q_LORA_RANK, -1).astype(BF16)
    q_ext = _rms_matmul(down, q_a_g, w_uq_ext, xcol=0, tm=2048, tn=2048, name="mla_uq")
    kv = _rms_matmul(down, kv_a_g, w_ukv, layer=i, xcol=1, tm=2048, tn=2048, name="mla_ukv")
    o = _mla_attention(q_ext.reshape(batch, seq, -1), kv.reshape(batch, seq, -1),
                       down.reshape(batch, seq, -1), tab, q_g, k_g, kp_col=lora // LANES)
    return _mm_res([o.reshape(batch * seq, -1)], w_o, x, layer=i, tk=2048, name="mla_out_proj")


def kernel(x, positions, ln_mix_g, ln_mlp_g, sf_w_in, sf_b_f, fox_q_g, fox_k_g, sf_w_o,
           mla_w_down, mla_q_a_g, mla_kv_a_g, mla_w_uq, mla_w_ukv, mla_q_g, mla_k_g,
           mla_w_o, mlp_w_up, mlp_w_down):
    batch, seq, d_model = x.shape
    depth = ln_mix_g.shape[0]
    tab = _rope_table(positions)
    sf_w_in, sf_w_o, mla_w_ukv, mla_w_o = (
        w.astype(BF16) for w in (sf_w_in, sf_w_o, mla_w_ukv, mla_w_o))
    h = x.reshape(batch * seq, d_model)
    for layer in range(depth):
        i = layer // 2
        if layer % 2 == 0:
            riders = (mlp_w_up, mlp_w_down) if layer == 0 else (None, None)
            h, casts = _sb_fox_layer(h, batch, seq, ln_mix_g[layer], sf_w_in, i, sf_b_f[i],
                                     fox_q_g[i], fox_k_g[i], sf_w_o, riders)
            if layer == 0:
                mlp_w_up, mlp_w_down = casts
        else:
            h = _mla_layer(h, batch, seq, tab, ln_mix_g[layer], mla_w_down[i], i, mla_q_a_g[i],
                           mla_kv_a_g[i], mla_w_uq[i], mla_w_ukv, mla_q_g[i], mla_k_g[i],
                           mla_w_o)
        h = _mlp(h, ln_mlp_g[layer], mlp_w_up, mlp_w_down, layer)
    return h.reshape(batch, seq, d_model)
```

```python
import functools

import jax
import jax.numpy as jnp
import numpy as np
from jax import lax
from jax.experimental import pallas as pl
from jax.experimental.pallas import tpu as pltpu

F32 = jnp.float32
BF16 = jnp.bfloat16

HEAD_DIM = 128
N_MLA_HEADS = 16
Q_LORA_RANK = 512
KV_LORA_RANK = 512
QK_NOPE_DIM = 128
QK_ROPE_DIM = 64
QK_HEAD_DIM = QK_NOPE_DIM + QK_ROPE_DIM
V_HEAD_DIM = 128
ROPE_THETA = 10000.0
EPS = 1e-6
LOG2E = 1.4426950408889634

LANES = 128
VMEM_LIMIT = 56 * 1024 * 1024
ARB = "arbitrary"


def _params(n_axes):
    return pltpu.CompilerParams(dimension_semantics=(ARB,) * n_axes,
                                vmem_limit_bytes=VMEM_LIMIT)


def _rms(x, g):
    ms = jnp.mean(x * x, axis=-1, keepdims=True)
    return x * lax.rsqrt(ms + EPS) * g


def _rms_matmul_kernel(x_ref, g_ref, w_ref, *rest, relu2, aux):
    if aux:
        wa_ref, o_ref, oa_ref, h_scr = rest
    else:
        o_ref, h_scr = rest

    @pl.when(pl.program_id(1) == 0)
    def _():
        h = _rms(x_ref[...].astype(F32), g_ref[...]).astype(BF16)
        h_scr[...] = h
        if aux:
            oa_ref[...] = jnp.dot(h, wa_ref[...], preferred_element_type=F32)

    acc = jnp.dot(h_scr[...], w_ref[...].astype(BF16), preferred_element_type=F32)
    if relu2:
        acc = jnp.square(jnp.maximum(acc, 0.0))
    o_ref[...] = acc.astype(o_ref.dtype)


def _weight_spec(w, layer, block, index):
    if w.ndim == 2:
        return pl.BlockSpec(block, index)
    return pl.BlockSpec((None,) + block, lambda *ids: (layer,) + index(*ids))


def _rms_matmul(x, g, w, *, layer=None, n=None, xcol=0, out_dtype=BF16, relu2=False,
                w_aux=None, tm=1024, tn=1024, name):
    t = x.shape[0]
    k = w.shape[-2]
    n = w.shape[-1] if n is None else n
    tm, tn = min(tm, t), min(tn, n)
    assert t % tm == 0 and n % tn == 0
    aux = w_aux is not None
    in_specs = [pl.BlockSpec((tm, k), lambda i, j: (i, xcol)),
                pl.BlockSpec((1, k), lambda i, j: (0, 0)),
                _weight_spec(w, layer, (k, tn), lambda i, j: (0, j))]
    out_specs = pl.BlockSpec((tm, tn), lambda i, j: (i, j))
    out_shape = jax.ShapeDtypeStruct((t, n), out_dtype)
    args = [x, g.reshape(1, k).astype(F32), w]
    if aux:
        na = w_aux.shape[1]
        in_specs.append(pl.BlockSpec((k, na), lambda i, j: (0, 0)))
        out_specs = [out_specs, pl.BlockSpec((tm, na), lambda i, j: (i, 0))]
        out_shape = [out_shape, jax.ShapeDtypeStruct((t, na), F32)]
        args.append(w_aux)
    return pl.pallas_call(
        functools.partial(_rms_matmul_kernel, relu2=relu2, aux=aux),
        grid=(t // tm, n // tn),
        in_specs=in_specs, out_specs=out_specs, out_shape=out_shape,
        scratch_shapes=[pltpu.VMEM((tm, k), BF16)],
        compiler_params=_params(2), name=name)(*args)


def _mm_res_kernel(*refs, n_pairs):
    a_refs = refs[:n_pairs]
    w_refs = refs[n_pairs:2 * n_pairs]
    r_ref, o_ref = refs[2 * n_pairs:]
    k = pl.program_id(2)

    @pl.when(k == 0)
    def _():
        o_ref[...] = r_ref[...]

    acc = jnp.dot(a_refs[0][...], w_refs[0][...].astype(BF16), preferred_element_type=F32)
    for a_ref, w_ref in zip(a_refs[1:], w_refs[1:]):
        acc += jnp.dot(a_ref[...], w_ref[...].astype(BF16), preferred_element_type=F32)
    o_ref[...] += acc


def _mm_res(a_list, w, r, *, layer=None, tm=1024, tn=1024, tk=1024, name):
    t, n = r.shape
    k = a_list[0].shape[1]
    tm, tn, tk = min(tm, t), min(tn, n), min(tk, k)
    assert t % tm == 0 and n % tn == 0 and k % tk == 0
    n_pairs = len(a_list)
    assert w.shape[-2] == n_pairs * k
    k_blocks = k // tk
    in_specs = ([pl.BlockSpec((tm, tk), lambda i, j, kk: (i, kk))] * n_pairs
                + [_weight_spec(w, layer, (tk, tn),
                                lambda i, j, kk, p=p: (p * k_blocks + kk, j))
                   for p in range(n_pairs)]
                + [pl.BlockSpec((tm, tn), lambda i, j, kk: (i, j))])
    return pl.pallas_call(
        functools.partial(_mm_res_kernel, n_pairs=n_pairs),
        grid=(t // tm, n // tn, k // tk),
        in_specs=in_specs,
        out_specs=pl.BlockSpec((tm, tn), lambda i, j, kk: (i, j)),
        out_shape=jax.ShapeDtypeStruct((t, n), F32),
        compiler_params=_params(3), name=name)(*a_list, *([w] * n_pairs), r)


def _log_sigmoid(z):
    return jnp.minimum(z, 0.0) - jnp.log(1.0 + jnp.exp(-jnp.abs(z)))


def _split3(x):
    x1 = x.astype(BF16)
    r1 = x - x1.astype(F32)
    x2 = r1.astype(BF16)
    x3 = (r1 - x2.astype(F32)).astype(BF16)
    return x1, x2, x3


def _forget_cumsum_kernel(f_ref, b_ref, o_ref, *, n_heads, chunk):
    s = f_ref.shape[0]
    row = lax.broadcasted_iota(jnp.int32, (chunk, chunk), 0)
    col = lax.broadcasted_iota(jnp.int32, (chunk, chunk), 1)
    lower = jnp.where(col <= row, 1.0, 0.0).astype(BF16)
    lane = lax.broadcasted_iota(jnp.int32, (chunk, LANES), 1)

    def body(c, carry):
        rows = pl.ds(pl.multiple_of(c * chunk, chunk), chunk)
        lf = jnp.where(lane < n_heads, _log_sigmoid(f_ref[rows, :] + b_ref[...]), 0.0)
        cs = carry
        for part in _split3(lf):
            cs = cs + jnp.dot(lower, part, preferred_element_type=F32)
        hi, mid, lo = _split3(-LOG2E * cs)
        packed = (hi.astype(F32) + pltpu.roll(mid.astype(F32), n_heads, 1)
                  + pltpu.roll(lo.astype(F32), 2 * n_heads, 1))
        o_ref[rows, :] = packed.astype(BF16)
        return cs[chunk - 1:chunk, :]

    lax.fori_loop(0, s // chunk, body, jnp.zeros((1, LANES), F32))


def _forget_cumsum(f_logit, b_pad, seq, n_heads):
    t = f_logit.shape[0]
    return pl.pallas_call(
        functools.partial(_forget_cumsum_kernel, n_heads=n_heads, chunk=min(512, seq)),
        grid=(t // seq,),
        in_specs=[pl.BlockSpec((seq, LANES), lambda b: (b, 0)),
                  pl.BlockSpec((1, LANES), lambda b: (0, 0))],
        out_specs=pl.BlockSpec((seq, LANES), lambda b: (b, 0)),
        out_shape=jax.ShapeDtypeStruct((t, LANES), BF16),
        compiler_params=_params(1), name="forget_cumsum")(f_logit, b_pad)


def _softmax_attention(qb, k_scr, vt_scr, o_ref, s_a, s_b, m_scr, l_scr, acc_scr, *, bq, bk):
    nk = bq // bk
    assert bq == nk * bk and nk % 2 == 0
    qi = pl.program_id(2)
    m_scr[...] = jnp.full(m_scr.shape, -jnp.inf, F32)
    l_scr[...] = jnp.zeros(l_scr.shape, F32)
    acc_scr[...] = jnp.zeros(acc_scr.shape, F32)
    n_full = nk * qi
    bufs = (s_a, s_b)

    def scores(c, q0):
        rows = pl.ds(pl.multiple_of(c * bk, bk), bk)
        return lax.dot_general(k_scr[rows, :], qb[q0:, :], (((1,), (1,)), ((), ())),
                               preferred_element_type=F32)

    def consume(s_ref, c, q0, q1, masked):
        s = s_ref[:, q0:q1]
        if masked:
            key = lax.broadcasted_iota(jnp.int32, s.shape, 0)
            query = lax.broadcasted_iota(jnp.int32, s.shape, 1)
            s = jnp.where(key <= query, s, -jnp.inf)
        m_prev = m_scr[:, q0:q1]
        m_new = jnp.maximum(m_prev, jnp.max(s, axis=0, keepdims=True))
        alpha = jnp.exp2(m_prev - m_new)
        p = jnp.exp2(s - m_new)
        l_scr[:, q0:q1] = alpha * l_scr[:, q0:q1] + jnp.sum(p, axis=0, keepdims=True)
        acc_scr[:, q0:q1] = alpha * acc_scr[:, q0:q1] + jnp.dot(
            vt_scr[c], p.astype(BF16), preferred_element_type=F32)
        m_scr[:, q0:q1] = m_new

    s_a[...] = scores(0, 0)

    def group(i, carry):
        c = nk * i
        for j in range(nk):
            bufs[(j + 1) % 2][...] = scores(c + j + 1, 0)
            consume(bufs[j % 2], c + j, 0, bq, False)
        return carry

    lax.fori_loop(0, qi, group, 0)
    for j in range(nk):
        if j + 1 < nk:
            bufs[(j + 1) % 2][:, (j + 1) * bk:] = scores(n_full + j + 1, (j + 1) * bk)
        consume(bufs[j % 2], n_full + j, j * bk, (j + 1) * bk, True)
        if j + 1 < nk:
            consume(bufs[j % 2], n_full + j, (j + 1) * bk, bq, False)
    o_ref[...] = (acc_scr[...] * (1.0 / l_scr[...])).T.astype(o_ref.dtype)


def _attn_scratch(seq, bq, bk, dk, dv):
    return [pltpu.VMEM((seq, dk), BF16), pltpu.VMEM((seq // bk, dv, bk), BF16),
            pltpu.VMEM((bk, bq), F32), pltpu.VMEM((bk, bq), F32),
            pltpu.VMEM((1, bq), F32), pltpu.VMEM((1, bq), F32), pltpu.VMEM((dv, bq), F32)]


def _lane_sumsq(x, n_valid):
    row = lax.broadcasted_iota(jnp.int32, (x.shape[-1], LANES), 0)
    sel = jnp.where(row < n_valid, 1.0, 0.0).astype(BF16)
    return jnp.dot((x * x).astype(BF16), sel, preferred_element_type=F32)


def _rms_mxu(x, g):
    return x * lax.rsqrt(_lane_sumsq(x, x.shape[-1]) / x.shape[-1] + EPS) * g


def _cast_rider(w, grid):
    n_layers, n_rows, n_cols = w.shape
    steps = grid[0] * grid[1] * grid[2]
    rows = n_layers * n_rows // steps
    per_layer = n_rows // rows
    assert rows * steps == n_layers * n_rows and rows % 16 == 0 and n_rows % rows == 0

    def index(a, b, c):
        step = (a * grid[1] + b) * grid[2] + c
        return step // per_layer, step % per_layer, 0

    return pl.BlockSpec((None, rows, n_cols), index), jax.ShapeDtypeStruct(w.shape, BF16)


def _prep_values(v_ref, vt_scr, c, rows):
    vt_scr[c] = v_ref[rows, :].T


def _fox_kernel(q_ref, k_ref, v_ref, fa_ref, gq_ref, gk_ref, *rest, bq, bk, n_heads, n_riders):
    w32_refs, o_ref, w16_refs = rest[:n_riders], rest[n_riders], rest[n_riders + 1:2 * n_riders + 1]
    k_scr, vt_scr, s_a, s_b, m_scr, l_scr, acc_scr = rest[2 * n_riders + 1:]
    d = HEAD_DIM
    h = pl.program_id(1)
    for w32_ref, w16_ref in zip(w32_refs, w16_refs):
        w16_ref[...] = w32_ref[...].astype(BF16)

    @pl.when(pl.program_id(2) == 0)
    def _():
        def prep(c, carry):
            rows = pl.ds(pl.multiple_of(c * bk, bk), bk)
            k_scr[rows, :d] = _rms_mxu(k_ref[rows, :].astype(F32), gk_ref[...]).astype(BF16)
            k_scr[rows, d:] = fa_ref[rows, :]
            _prep_values(v_ref, vt_scr, c, rows)
            return carry
        lax.fori_loop(0, k_ref.shape[0] // bk, prep, 0)

    qn = _rms_mxu(q_ref[...].astype(F32), gq_ref[...])
    lane = lax.broadcasted_iota(jnp.int32, (bq, LANES), 1)
    pick = (lane == h) | (lane == h + n_heads) | (lane == h + 2 * n_heads)
    qb = jnp.concatenate([qn.astype(BF16), jnp.where(pick, 1.0, 0.0).astype(BF16)], axis=-1)
    _softmax_attention(qb, k_scr, vt_scr, o_ref, s_a, s_b, m_scr, l_scr, acc_scr, bq=bq, bk=bk)


def _fox_attention(proj, f_aug, gq, gk, w32s, *, n_heads, q_col, k_col, v_col, bq=2048, bk=512):
    b, s, _ = proj.shape
    d = HEAD_DIM
    bq = min(bq, s)
    grid = (b, n_heads, s // bq)
    w_specs, w_shapes = zip(*[_cast_rider(w, grid) for w in w32s]) if w32s else ((), ())
    kv_spec = lambda col: pl.BlockSpec((None, s, d), lambda bi, h, qi: (bi, 0, col + h))
    out, *casts = pl.pallas_call(
        functools.partial(_fox_kernel, bq=bq, bk=bk, n_heads=n_heads, n_riders=len(w32s)),
        grid=grid,
        in_specs=[pl.BlockSpec((None, bq, d), lambda bi, h, qi: (bi, qi, q_col + h)),
                  kv_spec(k_col), kv_spec(v_col),
                  pl.BlockSpec((None, s, LANES), lambda bi, h, qi: (bi, 0, 0)),
                  pl.BlockSpec((1, d), lambda bi, h, qi: (0, 0)),
                  pl.BlockSpec((1, d), lambda bi, h, qi: (0, 0)), *w_specs],
        out_specs=[pl.BlockSpec((None, bq, d), lambda bi, h, qi: (bi, qi, h)), *w_specs],
        out_shape=[jax.ShapeDtypeStruct((b, s, n_heads * d), BF16), *w_shapes],
        scratch_shapes=_attn_scratch(s, bq, bk, 2 * LANES, d),
        compiler_params=_params(3), name="fox_attention")(
            proj, proj, proj, f_aug, (gq * (d ** -0.5 * LOG2E)).reshape(1, d).astype(F32),
            gk.reshape(1, d).astype(F32), *w32s)
    return out, casts


def _mla_norm_rope(x, tab, g_nope, g_pk, zero_pad):
    nope, pk = x[:, :QK_NOPE_DIM], x[:, QK_NOPE_DIM:]
    r = lax.rsqrt(_lane_sumsq(x, QK_HEAD_DIM) / QK_HEAD_DIM + EPS)
    a = pk * g_pk * tab
    rot = a + pltpu.roll(a, QK_ROPE_DIM, 1)
    if zero_pad:
        lane = lax.broadcasted_iota(jnp.int32, pk.shape, 1)
        rot = jnp.where(lane < QK_ROPE_DIM, rot, 0.0)
    return nope * r * g_nope, rot * r


def _mla_kernel(q_ref, kn_ref, v_ref, kp_ref, tabk_ref, tabq_ref,
                gqn_ref, gqp_ref, gkn_ref, gkp_ref, o_ref,
                k_scr, vt_scr, s_a, s_b, m_scr, l_scr, acc_scr, *, bq, bk):
    d = QK_NOPE_DIM

    @pl.when(pl.program_id(2) == 0)
    def _():
        def prep(c, carry):
            rows = pl.ds(pl.multiple_of(c * bk, bk), bk)
            kx = jnp.concatenate([kn_ref[rows, :].astype(F32), kp_ref[rows, :].astype(F32)],
                                 axis=-1)
            kn, kr = _mla_norm_rope(kx, tabk_ref[rows, :], gkn_ref[...], gkp_ref[...], True)
            k_scr[rows, :d] = kn.astype(BF16)
            k_scr[rows, d:] = kr.astype(BF16)
            _prep_values(v_ref, vt_scr, c, rows)
            return carry
        lax.fori_loop(0, kn_ref.shape[0] // bk, prep, 0)

    qn, qr = _mla_norm_rope(q_ref[...].astype(F32), tabq_ref[...], gqn_ref[...], gqp_ref[...],
                            False)
    qb = jnp.concatenate([qn.astype(BF16), qr.astype(BF16)], axis=-1)
    _softmax_attention(qb, k_scr, vt_scr, o_ref, s_a, s_b, m_scr, l_scr, acc_scr, bq=bq, bk=bk)


def _mla_attention(q_ext, kv, down, tab, gq, gk, *, kp_col, bq=2048, bk=512):
    b, s, _ = q_ext.shape
    h_n = N_MLA_HEADS
    d = QK_NOPE_DIM
    bq = min(bq, s)

    def pack_gain(g):
        g_pe = g[d:]
        half = QK_ROPE_DIM // 2
        g_sw = jnp.concatenate([g_pe[half:], g_pe[:half]])
        return (g[:d].reshape(1, d).astype(F32),
                jnp.concatenate([g_pe, g_sw]).reshape(1, LANES).astype(F32))

    gqn, gqp = pack_gain(gq * (QK_HEAD_DIM ** -0.5 * LOG2E))
    gkn, gkp = pack_gain(gk)
    full = lambda col_fn: pl.BlockSpec((None, s, LANES), lambda bi, h, qi: (bi, 0, col_fn(h)))
    vec = pl.BlockSpec((1, LANES), lambda bi, h, qi: (0, 0))
    return pl.pallas_call(
        functools.partial(_mla_kernel, bq=bq, bk=bk),
        grid=(b, h_n, s // bq),
        in_specs=[pl.BlockSpec((None, bq, 2 * LANES), lambda bi, h, qi: (bi, qi, h)),
                  full(lambda h: 2 * h), full(lambda h: 2 * h + 1),
                  full(lambda h: kp_col), full(lambda h: 0),
                  pl.BlockSpec((None, bq, LANES), lambda bi, h, qi: (bi, qi, 0)),
                  vec, vec, vec, vec],
        out_specs=pl.BlockSpec((None, bq, V_HEAD_DIM), lambda bi, h, qi: (bi, qi, h)),
        out_shape=jax.ShapeDtypeStruct((b, s, h_n * V_HEAD_DIM), BF16),
        scratch_shapes=_attn_scratch(s, bq, bk, 2 * LANES, V_HEAD_DIM),
        compiler_params=_params(3), name="mla_attention")(
            q_ext, kv, kv, down, tab, tab, gqn, gqp, gkn, gkp)


F32_EXP2_ZERO = -150.0


def _sb_kernel(q_ref, k_ref, v_ref, w32_ref, o_ref, w16_ref, vt_scr, r_scr, acc_scr, *,
               bq, bk, scale):
    assert bq == 2 * bk
    qi = pl.program_id(2)
    w16_ref[...] = w32_ref[...].astype(BF16)

    @pl.when(qi == 0)
    def _():
        def prep(c, carry):
            _prep_values(v_ref, vt_scr, c, pl.ds(pl.multiple_of(c * bk, bk), bk))
            return carry
        lax.fori_loop(0, k_ref.shape[0] // bk, prep, 0)

    qb = (q_ref[...].astype(F32) * (scale * LOG2E)).astype(BF16)
    row = lax.broadcasted_iota(jnp.int32, (bk, bk), 0)
    col = lax.broadcasted_iota(jnp.int32, (bk, bk), 1)
    later = jnp.where(col > row, 1.0, 0.0).astype(BF16)

    def scan_chunk(c, q0, q1, masked):
        rows = pl.ds(pl.multiple_of(c * bk, bk), bk)
        z = lax.dot_general(k_ref[rows, :], qb[q0:q1, :], (((1,), (1,)), ((), ())),
                            preferred_element_type=F32)
        log_beta = jnp.minimum(z, 0.0) - jnp.log2(1.0 + jnp.exp2(-jnp.abs(z)))
        log_rest = log_beta - z
        valid = None
        if masked:
            key = lax.broadcasted_iota(jnp.int32, z.shape, 0)
            query = lax.broadcasted_iota(jnp.int32, z.shape, 1)
            valid = key < query
            log_rest = jnp.where(valid, log_rest, 0.0)
        hi = log_rest.astype(BF16)
        lo = (log_rest - hi.astype(F32)).astype(BF16)
        suffix = (jnp.dot(later, hi, preferred_element_type=F32)
                  + jnp.dot(later, lo, preferred_element_type=F32))
        return log_beta + suffix, suffix[0:1, :] + log_rest[0:1, :], valid

    def weights(e, r, valid):
        w = jnp.exp2(e + r)
        if valid is not None:
            w = jnp.where(valid, w, 0.0)
        return w.astype(BF16)

    def window(with_low):
        e_top, t_top, v_top = scan_chunk(2 * qi + 1, bk, bq, True)
        e_mid, t_mid, v_mid = scan_chunk(2 * qi, 0, bq, True)
        if with_low:
            e_low, t_low, _ = scan_chunk(2 * qi - 1, 0, bk, False)
        w_top = weights(e_top, 0.0, v_top)
        r = jnp.concatenate([jnp.zeros((1, bk), F32), t_top], axis=1)
        w_mid = weights(e_mid, r, v_mid)
        r = r + t_mid
        acc = jnp.dot(vt_scr[2 * qi], w_mid, preferred_element_type=F32)
        acc_top = jnp.dot(vt_scr[2 * qi + 1], w_top, preferred_element_type=F32)
        acc_lo, r_lo = acc[:, :bk], r[:, :bk]
        if with_low:
            w_low = weights(e_low, r_lo, None)
            r_lo = r_lo + t_low
            acc_lo = acc_lo + jnp.dot(vt_scr[2 * qi - 1], w_low, preferred_element_type=F32)
        acc_scr[:, :bk] = acc_lo
        acc_scr[:, bk:] = acc[:, bk:] + acc_top
        r_scr[:, :bk] = r_lo
        r_scr[:, bk:] = r[:, bk:]

    def visit(c, q0, q1):
        e, t, _ = scan_chunk(c, q0, q1, False)
        r_prev = r_scr[:, q0:q1]
        acc_scr[:, q0:q1] += jnp.dot(vt_scr[c], weights(e, r_prev, None),
                                     preferred_element_type=F32)
        r_scr[:, q0:q1] = r_prev + t

    @pl.when(qi == 0)
    def _():
        window(False)

    @pl.when(qi > 0)
    def _():
        window(True)

    @pl.when(jnp.logical_and(qi > 0, jnp.max(r_scr[:, bk:]) > F32_EXP2_ZERO))
    def _():
        visit(2 * qi - 1, bk, bq)

    def more(c):
        return jnp.logical_and(c >= 0, jnp.max(r_scr[...]) > F32_EXP2_ZERO)

    def body(c):
        visit(c, 0, bq)
        return c - 1

    lax.while_loop(more, body, 2 * qi - 2)
    o_ref[...] = acc_scr[...].T.astype(o_ref.dtype)


def _sb_attention(proj, w32, *, n_heads, q_col, k_col, v_col, bq=512):
    b, s, _ = proj.shape
    d = HEAD_DIM
    bq = min(bq, s)
    bk = bq // 2
    grid = (b, n_heads, s // bq)
    if w32 is None:
        w32 = jnp.zeros((1, 16 * grid[0] * grid[1] * grid[2], LANES), F32)
    w_spec, w_shape = _cast_rider(w32, grid)
    kv_spec = lambda col: pl.BlockSpec((None, s, d), lambda bi, h, qi: (bi, 0, col + h))
    return pl.pallas_call(
        functools.partial(_sb_kernel, bq=bq, bk=bk, scale=d ** -0.5),
        grid=grid,
        in_specs=[pl.BlockSpec((None, bq, d), lambda bi, h, qi: (bi, qi, q_col + h)),
                  kv_spec(k_col), kv_spec(v_col), w_spec],
        out_specs=[pl.BlockSpec((None, bq, d), lambda bi, h, qi: (bi, qi, h)), w_spec],
        out_shape=[jax.ShapeDtypeStruct((b, s, n_heads * d), BF16), w_shape],
        scratch_shapes=[pltpu.VMEM((s // bk, d, bk), BF16), pltpu.VMEM((1, bq), F32),
                        pltpu.VMEM((d, bq), F32)],
        compiler_params=_params(3), name="sb_attention")(proj, proj, proj, w32)


def _rope_kernel(ang_ref, cos_ref, sin_ref):
    ang = ang_ref[...]
    cos_ref[...] = jnp.cos(ang)
    sin_ref[...] = jnp.sin(ang)


def _rope_table(positions):
    b, s = positions.shape
    half = QK_ROPE_DIM // 2
    per_row = LANES // half
    inv_freq = ROPE_THETA ** (-jnp.arange(half, dtype=F32) / half)
    pos = jnp.repeat(positions.astype(F32), half, axis=-1).reshape(b * s // per_row, LANES)
    ang_in = pos * jnp.tile(inv_freq, per_row)[None, :]
    rows = ang_in.shape[0]
    spec = pl.BlockSpec((rows, LANES), lambda i: (0, 0))
    cos, sin = pl.pallas_call(
        _rope_kernel, grid=(1,), in_specs=[spec], out_specs=[spec, spec],
        out_shape=[jax.ShapeDtypeStruct((rows, LANES), F32)] * 2,
        compiler_params=_params(1), name="rope_table")(ang_in)
    cos = cos.reshape(b, s, half)
    sin = sin.reshape(b, s, half)
    return jnp.concatenate([cos, cos, -sin, sin], axis=-1)


def _mlp(x, g, w_up, w_down, layer):
    a = _rms_matmul(x, g, w_up, layer=layer, relu2=True, tn=2048, name="mlp%d_up" % layer)
    return _mm_res([a], w_down, x, layer=layer, tk=4096, name="mlp%d_down" % layer)


def _sb_fox_layer(x, batch, seq, g, w_in, i, b_f, fox_q_g, fox_k_g, w_o, riders):
    d_model = x.shape[1]
    n_heads = d_model // (2 * HEAD_DIM)
    width = n_heads * HEAD_DIM
    w_f = jnp.pad(w_in[i, :, 6 * width:], ((0, 0), (0, LANES - n_heads))).astype(BF16)
    proj, f_logit = _rms_matmul(x, g, w_in, layer=i, n=6 * width, w_aux=w_f, tn=2048,
                                name="in_proj")
    b_pad = jnp.pad(b_f.astype(F32), (0, LANES - n_heads)).reshape(1, LANES)
    f_aug = _forget_cumsum(f_logit, b_pad, seq, n_heads).reshape(batch, seq, LANES)
    proj = proj.reshape(batch, seq, 6 * width)
    o_sb, cast_a = _sb_attention(proj, riders[0], n_heads=n_heads, q_col=0, k_col=n_heads,
                                 v_col=2 * n_heads)
    o_fx, casts_b = _fox_attention(proj, f_aug, fox_q_g, fox_k_g, riders[1], n_heads=n_heads,
                                   q_col=3 * n_heads, k_col=4 * n_heads, v_col=5 * n_heads)
    if w_o is None:
        w_o = casts_b[0]
    out = _mm_res([o_sb.reshape(-1, width), o_fx.reshape(-1, width)], w_o, x, layer=i,
                  name="sf_out_proj")
    return out, (cast_a, casts_b)


def _swap_halves(w):
    half = w.shape[-1] // 2
    return jnp.concatenate([w[..., half:], w[..., :half]], axis=-1)


def _mla_layer(x, batch, seq, tab, g, w_down, i, q_a_g, kv_a_g, w_uq, w_ukv, q_g, k_g, w_o):
    lora = Q_LORA_RANK + KV_LORA_RANK
    w_pe = w_down[:, lora:]
    w_down_ext = jnp.concatenate([w_down, _swap_halves(w_pe)], axis=1).astype(BF16)
    down = _rms_matmul(x, g, w_down_ext, out_dtype=F32, tn=w_down_ext.shape[1], name="mla_down")
    w_uq_h = w_uq.reshape(Q_LORA_RANK, N_MLA_HEADS, QK_HEAD_DIM)
    w_uq_ext = jnp.concatenate([w_uq_h, _swap_halves(w_uq_h[..., QK_NOPE_DIM:])], axis=-1)
    w_uq_ext = w_uq_ext.reshape(Q_LORA_RANK, -1).astype(BF16)
    q_ext = _rms_matmul(down, q_a_g, w_uq_ext, xcol=0, tm=2048, tn=2048, name="mla_uq")
    kv = _rms_matmul(down, kv_a_g, w_ukv, layer=i, xcol=1, tm=2048, tn=2048, name="mla_ukv")
    o = _mla_attention(q_ext.reshape(batch, seq, -1), kv.reshape(batch, seq, -1),
                       down.reshape(batch, seq, -1), tab, q_g, k_g, kp_col=lora // LANES)
    return _mm_res([o.reshape(batch * seq, -1)], w_o, x, layer=i, tk=2048, name="mla_out_proj")


def kernel(x, positions, ln_mix_g, ln_mlp_g, sf_w_in, sf_b_f, fox_q_g, fox_k_g, sf_w_o,
           mla_w_down, mla_q_a_g, mla_kv_a_g, mla_w_uq, mla_w_ukv, mla_q_g, mla_k_g,
           mla_w_o, mlp_w_up, mlp_w_down):
    batch, seq, d_model = x.shape
    depth = ln_mix_g.shape[0]
    tab = _rope_table(positions)
    sf_w_in, mla_w_ukv = sf_w_in.astype(BF16), mla_w_ukv.astype(BF16)
    sf_w_o16 = None
    h = x.reshape(batch * seq, d_model)
    for layer in range(depth):
        i = layer // 2
        if layer % 2 == 0:
            riders = (mlp_w_up, [sf_w_o, mlp_w_down, mla_w_o]) if layer == 0 else (None, [])
            h, casts = _sb_fox_layer(h, batch, seq, ln_mix_g[layer], sf_w_in, i, sf_b_f[i],
                                     fox_q_g[i], fox_k_g[i], sf_w_o16, riders)
            if layer == 0:
                mlp_w_up, (sf_w_o16, mlp_w_down, mla_w_o) = casts
        else:
            h = _mla_layer(h, batch, seq, tab, ln_mix_g[layer], mla_w_down[i], i, mla_q_a_g[i],
                           mla_kv_a_g[i], mla_w_uq[i], mla_w_ukv, mla_q_g[i], mla_k_g[i],
                           mla_w_o)
        h = _mlp(h, ln_mlp_g[layer], mlp_w_up, mlp_w_down, layer)
    return h.reshape(batch, seq, d_model)
```

```python
import functools

import jax
import jax.numpy as jnp
import numpy as np
from jax import lax
from jax.experimental import pallas as pl
from jax.experimental.pallas import tpu as pltpu

F32 = jnp.float32
BF16 = jnp.bfloat16

HEAD_DIM = 128
N_MLA_HEADS = 16
Q_LORA_RANK = 512
KV_LORA_RANK = 512
QK_NOPE_DIM = 128
QK_ROPE_DIM = 64
QK_HEAD_DIM = QK_NOPE_DIM + QK_ROPE_DIM
V_HEAD_DIM = 128
ROPE_THETA = 10000.0
EPS = 1e-6
LOG2E = 1.4426950408889634

LANES = 128
VMEM_LIMIT = 56 * 1024 * 1024
ARB = "arbitrary"


def _params(n_axes):
    return pltpu.CompilerParams(dimension_semantics=(ARB,) * n_axes,
                                vmem_limit_bytes=VMEM_LIMIT)


def _rms(x, g):
    ms = jnp.mean(x * x, axis=-1, keepdims=True)
    return x * lax.rsqrt(ms + EPS) * g


def _rms_matmul_kernel(x_ref, g_ref, w_ref, *rest, relu2, aux):
    if aux:
        wa_ref, o_ref, oa_ref, h_scr = rest
    else:
        o_ref, h_scr = rest

    @pl.when(pl.program_id(1) == 0)
    def _():
        h = _rms(x_ref[...].astype(F32), g_ref[...]).astype(BF16)
        h_scr[...] = h
        if aux:
            oa_ref[...] = jnp.dot(h, wa_ref[...], preferred_element_type=F32)

    acc = jnp.dot(h_scr[...], w_ref[...].astype(BF16), preferred_element_type=F32)
    if relu2:
        acc = jnp.square(jnp.maximum(acc, 0.0))
    o_ref[...] = acc.astype(o_ref.dtype)


def _weight_spec(w, layer, block, index):
    if w.ndim == 2:
        return pl.BlockSpec(block, index)
    return pl.BlockSpec((None,) + block, lambda *ids: (layer,) + index(*ids))


def _rms_matmul(x, g, w, *, layer=None, n=None, xcol=0, out_dtype=BF16, relu2=False,
                w_aux=None, tm=1024, tn=1024, name):
    t = x.shape[0]
    k = w.shape[-2]
    n = w.shape[-1] if n is None else n
    tm, tn = min(tm, t), min(tn, n)
    assert t % tm == 0 and n % tn == 0
    aux = w_aux is not None
    in_specs = [pl.BlockSpec((tm, k), lambda i, j: (i, xcol)),
                pl.BlockSpec((1, k), lambda i, j: (0, 0)),
                _weight_spec(w, layer, (k, tn), lambda i, j: (0, j))]
    out_specs = pl.BlockSpec((tm, tn), lambda i, j: (i, j))
    out_shape = jax.ShapeDtypeStruct((t, n), out_dtype)
    args = [x, g.reshape(1, k).astype(F32), w]
    if aux:
        na = w_aux.shape[1]
        in_specs.append(pl.BlockSpec((k, na), lambda i, j: (0, 0)))
        out_specs = [out_specs, pl.BlockSpec((tm, na), lambda i, j: (i, 0))]
        out_shape = [out_shape, jax.ShapeDtypeStruct((t, na), F32)]
        args.append(w_aux)
    return pl.pallas_call(
        functools.partial(_rms_matmul_kernel, relu2=relu2, aux=aux),
        grid=(t // tm, n // tn),
        in_specs=in_specs, out_specs=out_specs, out_shape=out_shape,
        scratch_shapes=[pltpu.VMEM((tm, k), BF16)],
        compiler_params=_params(2), name=name)(*args)


def _mm_res_kernel(*refs, n_pairs):
    a_refs = refs[:n_pairs]
    w_refs = refs[n_pairs:2 * n_pairs]
    r_ref, o_ref = refs[2 * n_pairs:]
    k = pl.program_id(2)

    @pl.when(k == 0)
    def _():
        o_ref[...] = r_ref[...]

    acc = jnp.dot(a_refs[0][...], w_refs[0][...].astype(BF16), preferred_element_type=F32)
    for a_ref, w_ref in zip(a_refs[1:], w_refs[1:]):
        acc += jnp.dot(a_ref[...], w_ref[...].astype(BF16), preferred_element_type=F32)
    o_ref[...] += acc


def _mm_res(a_list, w, r, *, layer=None, tm=1024, tn=1024, tk=1024, name):
    t, n = r.shape
    k = a_list[0].shape[1]
    tm, tn, tk = min(tm, t), min(tn, n), min(tk, k)
    assert t % tm == 0 and n % tn == 0 and k % tk == 0
    n_pairs = len(a_list)
    assert w.shape[-2] == n_pairs * k
    k_blocks = k // tk
    in_specs = ([pl.BlockSpec((tm, tk), lambda i, j, kk: (i, kk))] * n_pairs
                + [_weight_spec(w, layer, (tk, tn),
                                lambda i, j, kk, p=p: (p * k_blocks + kk, j))
                   for p in range(n_pairs)]
                + [pl.BlockSpec((tm, tn), lambda i, j, kk: (i, j))])
    return pl.pallas_call(
        functools.partial(_mm_res_kernel, n_pairs=n_pairs),
        grid=(t // tm, n // tn, k // tk),
        in_specs=in_specs,
        out_specs=pl.BlockSpec((tm, tn), lambda i, j, kk: (i, j)),
        out_shape=jax.ShapeDtypeStruct((t, n), F32),
        compiler_params=_params(3), name=name)(*a_list, *([w] * n_pairs), r)


def _log_sigmoid(z):
    return jnp.minimum(z, 0.0) - jnp.log(1.0 + jnp.exp(-jnp.abs(z)))


def _split3(x):
    x1 = x.astype(BF16)
    r1 = x - x1.astype(F32)
    x2 = r1.astype(BF16)
    x3 = (r1 - x2.astype(F32)).astype(BF16)
    return x1, x2, x3


def _forget_cumsum_kernel(f_ref, b_ref, o_ref, *, n_heads, chunk):
    s = f_ref.shape[0]
    row = lax.broadcasted_iota(jnp.int32, (chunk, chunk), 0)
    col = lax.broadcasted_iota(jnp.int32, (chunk, chunk), 1)
    lower = jnp.where(col <= row, 1.0, 0.0).astype(BF16)
    lane = lax.broadcasted_iota(jnp.int32, (chunk, LANES), 1)

    def body(c, carry):
        rows = pl.ds(pl.multiple_of(c * chunk, chunk), chunk)
        lf = jnp.where(lane < n_heads, _log_sigmoid(f_ref[rows, :] + b_ref[...]), 0.0)
        cs = carry
        for part in _split3(lf):
            cs = cs + jnp.dot(lower, part, preferred_element_type=F32)
        hi, mid, lo = _split3(-LOG2E * cs)
        packed = (hi.astype(F32) + pltpu.roll(mid.astype(F32), n_heads, 1)
                  + pltpu.roll(lo.astype(F32), 2 * n_heads, 1))
        o_ref[rows, :] = packed.astype(BF16)
        return cs[chunk - 1:chunk, :]

    lax.fori_loop(0, s // chunk, body, jnp.zeros((1, LANES), F32))


def _forget_cumsum(f_logit, b_pad, seq, n_heads):
    t = f_logit.shape[0]
    return pl.pallas_call(
        functools.partial(_forget_cumsum_kernel, n_heads=n_heads, chunk=min(512, seq)),
        grid=(t // seq,),
        in_specs=[pl.BlockSpec((seq, LANES), lambda b: (b, 0)),
                  pl.BlockSpec((1, LANES), lambda b: (0, 0))],
        out_specs=pl.BlockSpec((seq, LANES), lambda b: (b, 0)),
        out_shape=jax.ShapeDtypeStruct((t, LANES), BF16),
        compiler_params=_params(1), name="forget_cumsum")(f_logit, b_pad)


def _softmax_attention(qb, k_scr, vt_scr, o_ref, s_a, s_b, m_scr, l_scr, acc_scr, *, bq, bk):
    nk = bq // bk
    assert bq == nk * bk and nk % 2 == 0
    qi = pl.program_id(2)
    m_scr[...] = jnp.full(m_scr.shape, -jnp.inf, F32)
    l_scr[...] = jnp.zeros(l_scr.shape, F32)
    acc_scr[...] = jnp.zeros(acc_scr.shape, F32)
    n_full = nk * qi
    bufs = (s_a, s_b)

    def scores(c, q0):
        rows = pl.ds(pl.multiple_of(c * bk, bk), bk)
        return lax.dot_general(k_scr[rows, :], qb[q0:, :], (((1,), (1,)), ((), ())),
                               preferred_element_type=F32)

    def consume(s_ref, c, q0, q1, masked):
        s = s_ref[:, q0:q1]
        if masked:
            key = lax.broadcasted_iota(jnp.int32, s.shape, 0)
            query = lax.broadcasted_iota(jnp.int32, s.shape, 1)
            s = jnp.where(key <= query, s, -jnp.inf)
        m_prev = m_scr[:, q0:q1]
        m_new = jnp.maximum(m_prev, jnp.max(s, axis=0, keepdims=True))
        alpha = jnp.exp2(m_prev - m_new)
        p = jnp.exp2(s - m_new)
        l_scr[:, q0:q1] = alpha * l_scr[:, q0:q1] + jnp.sum(p, axis=0, keepdims=True)
        acc_scr[:, q0:q1] = alpha * acc_scr[:, q0:q1] + jnp.dot(
            vt_scr[c], p.astype(BF16), preferred_element_type=F32)
        m_scr[:, q0:q1] = m_new

    s_a[...] = scores(0, 0)

    def group(i, carry):
        c = nk * i
        for j in range(nk):
            bufs[(j + 1) % 2][...] = scores(c + j + 1, 0)
            consume(bufs[j % 2], c + j, 0, bq, False)
        return carry

    lax.fori_loop(0, qi, group, 0)
    for j in range(nk):
        if j + 1 < nk:
            bufs[(j + 1) % 2][:, (j + 1) * bk:] = scores(n_full + j + 1, (j + 1) * bk)
        consume(bufs[j % 2], n_full + j, j * bk, (j + 1) * bk, True)
        if j + 1 < nk:
            consume(bufs[j % 2], n_full + j, (j + 1) * bk, bq, False)
    o_ref[...] = (acc_scr[...] * (1.0 / l_scr[...])).T.astype(o_ref.dtype)


def _attn_scratch(seq, bq, bk, dk, dv):
    return [pltpu.VMEM((seq, dk), BF16), pltpu.VMEM((seq // bk, dv, bk), BF16),
            pltpu.VMEM((bk, bq), F32), pltpu.VMEM((bk, bq), F32),
            pltpu.VMEM((1, bq), F32), pltpu.VMEM((1, bq), F32), pltpu.VMEM((dv, bq), F32)]


def _lane_sumsq(x, n_valid):
    row = lax.broadcasted_iota(jnp.int32, (x.shape[-1], LANES), 0)
    sel = jnp.where(row < n_valid, 1.0, 0.0).astype(BF16)
    return jnp.dot((x * x).astype(BF16), sel, preferred_element_type=F32)


def _rms_mxu(x, g):
    return x * lax.rsqrt(_lane_sumsq(x, x.shape[-1]) / x.shape[-1] + EPS) * g


def _cast_rider(w, grid):
    n_layers, n_rows, n_cols = w.shape
    steps = grid[0] * grid[1] * grid[2]
    rows = n_layers * n_rows // steps
    per_layer = n_rows // rows
    assert rows * steps == n_layers * n_rows and rows % 16 == 0 and n_rows % rows == 0

    def index(a, b, c):
        step = (a * grid[1] + b) * grid[2] + c
        return step // per_layer, step % per_layer, 0

    return pl.BlockSpec((None, rows, n_cols), index), jax.ShapeDtypeStruct(w.shape, BF16)


def _prep_values(v_ref, vt_scr, c, rows):
    vt_scr[c] = v_ref[rows, :].T


def _fox_kernel(q_ref, k_ref, v_ref, fa_ref, gq_ref, gk_ref, w32_ref, o_ref, w16_ref,
                k_scr, vt_scr, s_a, s_b, m_scr, l_scr, acc_scr, *, bq, bk, n_heads):
    d = HEAD_DIM
    h = pl.program_id(1)
    w16_ref[...] = w32_ref[...].astype(BF16)

    @pl.when(pl.program_id(2) == 0)
    def _():
        def prep(c, carry):
            rows = pl.ds(pl.multiple_of(c * bk, bk), bk)
            k_scr[rows, :d] = _rms_mxu(k_ref[rows, :].astype(F32), gk_ref[...]).astype(BF16)
            k_scr[rows, d:] = fa_ref[rows, :]
            _prep_values(v_ref, vt_scr, c, rows)
            return carry
        lax.fori_loop(0, k_ref.shape[0] // bk, prep, 0)

    qn = _rms_mxu(q_ref[...].astype(F32), gq_ref[...])
    lane = lax.broadcasted_iota(jnp.int32, (bq, LANES), 1)
    pick = (lane == h) | (lane == h + n_heads) | (lane == h + 2 * n_heads)
    qb = jnp.concatenate([qn.astype(BF16), jnp.where(pick, 1.0, 0.0).astype(BF16)], axis=-1)
    _softmax_attention(qb, k_scr, vt_scr, o_ref, s_a, s_b, m_scr, l_scr, acc_scr, bq=bq, bk=bk)


def _fox_attention(proj, f_aug, gq, gk, w32, *, n_heads, q_col, k_col, v_col, bq=2048, bk=512):
    b, s, _ = proj.shape
    d = HEAD_DIM
    bq = min(bq, s)
    grid = (b, n_heads, s // bq)
    if w32 is None:
        w32 = jnp.zeros((1, 16 * grid[0] * grid[1] * grid[2], LANES), F32)
    w_spec, w_shape = _cast_rider(w32, grid)
    kv_spec = lambda col: pl.BlockSpec((None, s, d), lambda bi, h, qi: (bi, 0, col + h))
    return pl.pallas_call(
        functools.partial(_fox_kernel, bq=bq, bk=bk, n_heads=n_heads),
        grid=grid,
        in_specs=[pl.BlockSpec((None, bq, d), lambda bi, h, qi: (bi, qi, q_col + h)),
                  kv_spec(k_col), kv_spec(v_col),
                  pl.BlockSpec((None, s, LANES), lambda bi, h, qi: (bi, 0, 0)),
                  pl.BlockSpec((1, d), lambda bi, h, qi: (0, 0)),
                  pl.BlockSpec((1, d), lambda bi, h, qi: (0, 0)), w_spec],
        out_specs=[pl.BlockSpec((None, bq, d), lambda bi, h, qi: (bi, qi, h)), w_spec],
        out_shape=[jax.ShapeDtypeStruct((b, s, n_heads * d), BF16), w_shape],
        scratch_shapes=_attn_scratch(s, bq, bk, 2 * LANES, d),
        compiler_params=_params(3), name="fox_attention")(
            proj, proj, proj, f_aug, (gq * (d ** -0.5 * LOG2E)).reshape(1, d).astype(F32),
            gk.reshape(1, d).astype(F32), w32)


def _mla_norm_rope(x, tab, g_nope, g_pk, zero_pad):
    nope, pk = x[:, :QK_NOPE_DIM], x[:, QK_NOPE_DIM:]
    r = lax.rsqrt(_lane_sumsq(x, QK_HEAD_DIM) / QK_HEAD_DIM + EPS)
    a = pk * g_pk * tab
    rot = a + pltpu.roll(a, QK_ROPE_DIM, 1)
    if zero_pad:
        lane = lax.broadcasted_iota(jnp.int32, pk.shape, 1)
        rot = jnp.where(lane < QK_ROPE_DIM, rot, 0.0)
    return nope * r * g_nope, rot * r


def _mla_kernel(q_ref, kn_ref, v_ref, kp_ref, tabk_ref, tabq_ref,
                gqn_ref, gqp_ref, gkn_ref, gkp_ref, o_ref,
                k_scr, vt_scr, s_a, s_b, m_scr, l_scr, acc_scr, *, bq, bk):
    d = QK_NOPE_DIM

    @pl.when(pl.program_id(2) == 0)
    def _():
        def prep(c, carry):
            rows = pl.ds(pl.multiple_of(c * bk, bk), bk)
            kx = jnp.concatenate([kn_ref[rows, :].astype(F32), kp_ref[rows, :].astype(F32)],
                                 axis=-1)
            kn, kr = _mla_norm_rope(kx, tabk_ref[rows, :], gkn_ref[...], gkp_ref[...], True)
            k_scr[rows, :d] = kn.astype(BF16)
            k_scr[rows, d:] = kr.astype(BF16)
            _prep_values(v_ref, vt_scr, c, rows)
            return carry
        lax.fori_loop(0, kn_ref.shape[0] // bk, prep, 0)

    qn, qr = _mla_norm_rope(q_ref[...].astype(F32), tabq_ref[...], gqn_ref[...], gqp_ref[...],
                            False)
    qb = jnp.concatenate([qn.astype(BF16), qr.astype(BF16)], axis=-1)
    _softmax_attention(qb, k_scr, vt_scr, o_ref, s_a, s_b, m_scr, l_scr, acc_scr, bq=bq, bk=bk)


def _mla_attention(q_ext, kv, down, tab, gq, gk, *, kp_col, bq=2048, bk=512):
    b, s, _ = q_ext.shape
    h_n = N_MLA_HEADS
    d = QK_NOPE_DIM
    bq = min(bq, s)

    def pack_gain(g):
        g_pe = g[d:]
        half = QK_ROPE_DIM // 2
        g_sw = jnp.concatenate([g_pe[half:], g_pe[:half]])
        return (g[:d].reshape(1, d).astype(F32),
                jnp.concatenate([g_pe, g_sw]).reshape(1, LANES).astype(F32))

    gqn, gqp = pack_gain(gq * (QK_HEAD_DIM ** -0.5 * LOG2E))
    gkn, gkp = pack_gain(gk)
    full = lambda col_fn: pl.BlockSpec((None, s, LANES), lambda bi, h, qi: (bi, 0, col_fn(h)))
    vec = pl.BlockSpec((1, LANES), lambda bi, h, qi: (0, 0))
    return pl.pallas_call(
        functools.partial(_mla_kernel, bq=bq, bk=bk),
        grid=(b, h_n, s // bq),
        in_specs=[pl.BlockSpec((None, bq, 2 * LANES), lambda bi, h, qi: (bi, qi, h)),
                  full(lambda h: 2 * h), full(lambda h: 2 * h + 1),
                  full(lambda h: kp_col), full(lambda h: 0),
                  pl.BlockSpec((None, bq, LANES), lambda bi, h, qi: (bi, qi, 0)),
                  vec, vec, vec, vec],
        out_specs=pl.BlockSpec((None, bq, V_HEAD_DIM), lambda bi, h, qi: (bi, qi, h)),
        out_shape=jax.ShapeDtypeStruct((b, s, h_n * V_HEAD_DIM), BF16),
        scratch_shapes=_attn_scratch(s, bq, bk, 2 * LANES, V_HEAD_DIM),
        compiler_params=_params(3), name="mla_attention")(
            q_ext, kv, kv, down, tab, tab, gqn, gqp, gkn, gkp)


F32_EXP2_ZERO = -150.0


def _sb_kernel(q_ref, k_ref, v_ref, w32_ref, o_ref, w16_ref, vt_scr, r_scr, acc_scr, *,
               bq, bk, scale):
    assert bq == 2 * bk
    qi = pl.program_id(2)
    w16_ref[...] = w32_ref[...].astype(BF16)

    @pl.when(qi == 0)
    def _():
        def prep(c, carry):
            _prep_values(v_ref, vt_scr, c, pl.ds(pl.multiple_of(c * bk, bk), bk))
            return carry
        lax.fori_loop(0, k_ref.shape[0] // bk, prep, 0)

    qb = (q_ref[...].astype(F32) * (scale * LOG2E)).astype(BF16)
    row = lax.broadcasted_iota(jnp.int32, (bk, bk), 0)
    col = lax.broadcasted_iota(jnp.int32, (bk, bk), 1)
    later = jnp.where(col > row, 1.0, 0.0).astype(BF16)

    def scan_chunk(c, q0, q1, masked):
        rows = pl.ds(pl.multiple_of(c * bk, bk), bk)
        z = lax.dot_general(k_ref[rows, :], qb[q0:q1, :], (((1,), (1,)), ((), ())),
                            preferred_element_type=F32)
        log_beta = jnp.minimum(z, 0.0) - jnp.log2(1.0 + jnp.exp2(-jnp.abs(z)))
        log_rest = log_beta - z
        valid = None
        if masked:
            key = lax.broadcasted_iota(jnp.int32, z.shape, 0)
            query = lax.broadcasted_iota(jnp.int32, z.shape, 1)
            valid = key < query
            log_rest = jnp.where(valid, log_rest, 0.0)
        hi = log_rest.astype(BF16)
        lo = (log_rest - hi.astype(F32)).astype(BF16)
        suffix = (jnp.dot(later, hi, preferred_element_type=F32)
                  + jnp.dot(later, lo, preferred_element_type=F32))
        return log_beta + suffix, suffix[0:1, :] + log_rest[0:1, :], valid

    def weights(e, r, valid):
        w = jnp.exp2(e + r)
        if valid is not None:
            w = jnp.where(valid, w, 0.0)
        return w.astype(BF16)

    def window(with_low):
        e_top, t_top, v_top = scan_chunk(2 * qi + 1, bk, bq, True)
        e_mid, t_mid, v_mid = scan_chunk(2 * qi, 0, bq, True)
        if with_low:
            e_low, t_low, _ = scan_chunk(2 * qi - 1, 0, bk, False)
        w_top = weights(e_top, 0.0, v_top)
        r = jnp.concatenate([jnp.zeros((1, bk), F32), t_top], axis=1)
        w_mid = weights(e_mid, r, v_mid)
        r = r + t_mid
        acc = jnp.dot(vt_scr[2 * qi], w_mid, preferred_element_type=F32)
        acc_top = jnp.dot(vt_scr[2 * qi + 1], w_top, preferred_element_type=F32)
        acc_lo, r_lo = acc[:, :bk], r[:, :bk]
        if with_low:
            w_low = weights(e_low, r_lo, None)
            r_lo = r_lo + t_low
            acc_lo = acc_lo + jnp.dot(vt_scr[2 * qi - 1], w_low, preferred_element_type=F32)
        acc_scr[:, :bk] = acc_lo
        acc_scr[:, bk:] = acc[:, bk:] + acc_top
        r_scr[:, :bk] = r_lo
        r_scr[:, bk:] = r[:, bk:]

    def visit(c, q0, q1):
        e, t, _ = scan_chunk(c, q0, q1, False)
        r_prev = r_scr[:, q0:q1]
        acc_scr[:, q0:q1] += jnp.dot(vt_scr[c], weights(e, r_prev, None),
                                     preferred_element_type=F32)
        r_scr[:, q0:q1] = r_prev + t

    @pl.when(qi == 0)
    def _():
        window(False)

    @pl.when(qi > 0)
    def _():
        window(True)

    @pl.when(jnp.logical_and(qi > 0, jnp.max(r_scr[:, bk:]) > F32_EXP2_ZERO))
    def _():
        visit(2 * qi - 1, bk, bq)

    def more(c):
        return jnp.logical_and(c >= 0, jnp.max(r_scr[...]) > F32_EXP2_ZERO)

    def body(c):
        visit(c, 0, bq)
        return c - 1

    lax.while_loop(more, body, 2 * qi - 2)
    o_ref[...] = acc_scr[...].T.astype(o_ref.dtype)


def _sb_attention(proj, w32, *, n_heads, q_col, k_col, v_col, bq=512):
    b, s, _ = proj.shape
    d = HEAD_DIM
    bq = min(bq, s)
    bk = bq // 2
    grid = (b, n_heads, s // bq)
    if w32 is None:
        w32 = jnp.zeros((1, 16 * grid[0] * grid[1] * grid[2], LANES), F32)
    w_spec, w_shape = _cast_rider(w32, grid)
    kv_spec = lambda col: pl.BlockSpec((None, s, d), lambda bi, h, qi: (bi, 0, col + h))
    return pl.pallas_call(
        functools.partial(_sb_kernel, bq=bq, bk=bk, scale=d ** -0.5),
        grid=grid,
        in_specs=[pl.BlockSpec((None, bq, d), lambda bi, h, qi: (bi, qi, q_col + h)),
                  kv_spec(k_col), kv_spec(v_col), w_spec],
        out_specs=[pl.BlockSpec((None, bq, d), lambda bi, h, qi: (bi, qi, h)), w_spec],
        out_shape=[jax.ShapeDtypeStruct((b, s, n_heads * d), BF16), w_shape],
        scratch_shapes=[pltpu.VMEM((s // bk, d, bk), BF16), pltpu.VMEM((1, bq), F32),
                        pltpu.VMEM((d, bq), F32)],
        compiler_params=_params(3), name="sb_attention")(proj, proj, proj, w32)


def _rope_kernel(ang_ref, cos_ref, sin_ref):
    ang = ang_ref[...]
    cos_ref[...] = jnp.cos(ang)
    sin_ref[...] = jnp.sin(ang)


def _rope_table(positions):
    b, s = positions.shape
    half = QK_ROPE_DIM // 2
    per_row = LANES // half
    inv_freq = ROPE_THETA ** (-jnp.arange(half, dtype=F32) / half)
    pos = jnp.repeat(positions.astype(F32), half, axis=-1).reshape(b * s // per_row, LANES)
    ang_in = pos * jnp.tile(inv_freq, per_row)[None, :]
    rows = ang_in.shape[0]
    spec = pl.BlockSpec((rows, LANES), lambda i: (0, 0))
    cos, sin = pl.pallas_call(
        _rope_kernel, grid=(1,), in_specs=[spec], out_specs=[spec, spec],
        out_shape=[jax.ShapeDtypeStruct((rows, LANES), F32)] * 2,
        compiler_params=_params(1), name="rope_table")(ang_in)
    cos = cos.reshape(b, s, half)
    sin = sin.reshape(b, s, half)
    return jnp.concatenate([cos, cos, -sin, sin], axis=-1)


def _mlp(x, g, w_up, w_down, layer):
    a = _rms_matmul(x, g, w_up, layer=layer, relu2=True, tn=2048, name="mlp%d_up" % layer)
    return _mm_res([a], w_down, x, layer=layer, tk=4096, name="mlp%d_down" % layer)


def _sb_fox_layer(x, batch, seq, g, w_in, i, b_f, fox_q_g, fox_k_g, w_o, riders):
    d_model = x.shape[1]
    n_heads = d_model // (2 * HEAD_DIM)
    width = n_heads * HEAD_DIM
    w_f = jnp.pad(w_in[i, :, 6 * width:], ((0, 0), (0, LANES - n_heads))).astype(BF16)
    proj, f_logit = _rms_matmul(x, g, w_in, layer=i, n=6 * width, w_aux=w_f, tn=2048,
                                name="in_proj")
    b_pad = jnp.pad(b_f.astype(F32), (0, LANES - n_heads)).reshape(1, LANES)
    f_aug = _forget_cumsum(f_logit, b_pad, seq, n_heads).reshape(batch, seq, LANES)
    proj = proj.reshape(batch, seq, 6 * width)
    o_sb, cast_a = _sb_attention(proj, riders[0], n_heads=n_heads, q_col=0, k_col=n_heads,
                                 v_col=2 * n_heads)
    o_fx, cast_b = _fox_attention(proj, f_aug, fox_q_g, fox_k_g, riders[1], n_heads=n_heads,
                                  q_col=3 * n_heads, k_col=4 * n_heads, v_col=5 * n_heads)
    out = _mm_res([o_sb.reshape(-1, width), o_fx.reshape(-1, width)], w_o, x, layer=i,
                  name="sf_out_proj")
    return out, (cast_a, cast_b)


def _swap_halves(w):
    half = w.shape[-1] // 2
    return jnp.concatenate([w[..., half:], w[..., :half]], axis=-1)


def _mla_layer(x, batch, seq, tab, g, w_down, i, q_a_g, kv_a_g, w_uq, w_ukv, q_g, k_g, w_o):
    lora = Q_LORA_RANK + KV_LORA_RANK
    w_pe = w_down[:, lora:]
    w_down_ext = jnp.concatenate([w_down, _swap_halves(w_pe)], axis=1).astype(BF16)
    down = _rms_matmul(x, g, w_down_ext, out_dtype=F32, tn=w_down_ext.shape[1], name="mla_down")
    w_uq_h = w_uq.reshape(Q_LORA_RANK, N_MLA_HEADS, QK_HEAD_DIM)
    w_uq_ext = jnp.concatenate([w_uq_h, _swap_halves(w_uq_h[..., QK_NOPE_DIM:])], axis=-1)
    w_uq_ext = w_uq_ext.reshape(Q_LORA_RANK, -1).astype(BF16)
    q_ext = _rms_matmul(down, q_a_g, w_uq_ext, xcol=0, tm=2048, tn=2048, name="mla_uq")
    kv = _rms_matmul(down, kv_a_g, w_ukv, layer=i, xcol=1, tm=2048, tn=2048, name="mla_ukv")
    o = _mla_attention(q_ext.reshape(batch, seq, -1), kv.reshape(batch, seq, -1),
                       down.reshape(batch, seq, -1), tab, q_g, k_g, kp_col=lora // LANES)
    return _mm_res([o.reshape(batch * seq, -1)], w_o, x, layer=i, tn=2048, tk=1024,
                   name="mla_out_proj")


def kernel(x, positions, ln_mix_g, ln_mlp_g, sf_w_in, sf_b_f, fox_q_g, fox_k_g, sf_w_o,
           mla_w_down, mla_q_a_g, mla_kv_a_g, mla_w_uq, mla_w_ukv, mla_q_g, mla_k_g,
           mla_w_o, mlp_w_up, mlp_w_down):
    batch, seq, d_model = x.shape
    depth = ln_mix_g.shape[0]
    tab = _rope_table(positions)
    sf_w_in, sf_w_o, mla_w_ukv, mla_w_o = (
        w.astype(BF16) for w in (sf_w_in, sf_w_o, mla_w_ukv, mla_w_o))
    h = x.reshape(batch * seq, d_model)
    for layer in range(depth):
        i = layer // 2
        if layer % 2 == 0:
            riders = (mlp_w_up, mlp_w_down) if layer == 0 else (None, None)
            h, casts = _sb_fox_layer(h, batch, seq, ln_mix_g[layer], sf_w_in, i, sf_b_f[i],
                                     fox_q_g[i], fox_k_g[i], sf_w_o, riders)
            if layer == 0:
                mlp_w_up, mlp_w_down = casts
        else:
            h = _mla_layer(h, batch, seq, tab, ln_mix_g[layer], mla_w_down[i], i, mla_q_a_g[i],
                           mla_kv_a_g[i], mla_w_uq[i], mla_w_ukv, mla_q_g[i], mla_k_g[i],
                           mla_w_o)
        h = _mlp(h, ln_mlp_g[layer], mlp_w_up, mlp_w_down, layer)
    return h.reshape(batch, seq, d_model)
```
